```python
import math
import jax, jax.numpy as jnp
from jax import lax
import numpy as np

D_MODEL = 1024
BATCH = 8
SEQ = 2048
DEPTH = 2

CONV_WIDTH = D_MODEL // 2
CONV_GROUPS = 8
CONV_K = 3
N_HEADS = 8
HEAD_DIM = 64
ATTN_WIDTH = N_HEADS * HEAD_DIM
D_FF = 2816
Q_BLOCK = 128
RMS_EPS = 1e-6
N_BRANCH = 2
IN_COLS = 3 * CONV_WIDTH + 3 * ATTN_WIDTH + N_BRANCH * D_MODEL

kernel_name = "hybrid_shortconv_stickbreaking_macaron"


def rmsnorm(x, g):
    xf = x.astype(jnp.float32)
    y = xf * lax.rsqrt(jnp.mean(xf * xf, axis=-1, keepdims=True) + RMS_EPS)
    return (y * g.astype(jnp.float32)).astype(x.dtype)


def swiglu(h, w13, w2):
    gu = h @ w13
    a, b = jnp.split(gu, 2, axis=-1)
    return (jax.nn.silu(a) * b) @ w2


def causal_depthwise_conv(u, w):
    c = u.shape[-1]
    return lax.conv_general_dilated(
        u, w[:, None, :].astype(u.dtype), window_strides=(1,), padding=[(CONV_K - 1, 0)],
        dimension_numbers=("NWC", "WIO", "NWC"), feature_group_count=c)


def stick_breaking_attention(q, k, v):
    b, s, h, dh = q.shape
    q = jnp.transpose(q, (0, 2, 1, 3))
    k = jnp.transpose(k, (0, 2, 1, 3))
    v = jnp.transpose(v, (0, 2, 1, 3))
    scale = 1.0 / math.sqrt(dh)
    outs = []
    for i in range(s // Q_BLOCK):
        q0 = i * Q_BLOCK
        end = q0 + Q_BLOCK
        qb = q[:, :, q0:end]
        kb = k[:, :, :end]
        vb = v[:, :, :end]
        z = jnp.einsum("bhqd,bhkd->bhqk", qb, kb).astype(jnp.float32) * scale
        t_pos = q0 + jnp.arange(Q_BLOCK)[:, None]
        s_pos = jnp.arange(end)[None, :]
        causal = s_pos < t_pos
        log_keep = jnp.where(causal, jax.nn.log_sigmoid(-z), 0.0)
        shifted = jnp.concatenate([log_keep[..., 1:], jnp.zeros_like(log_keep[..., :1])], axis=-1)
        excl = lax.cumsum(shifted, axis=shifted.ndim - 1, reverse=True)
        log_a = jax.nn.log_sigmoid(z) + excl
        a = jnp.where(causal, jnp.exp(log_a), 0.0)
        outs.append(jnp.einsum("bhqk,bhkd->bhqd", a.astype(vb.dtype), vb))
    o = jnp.concatenate(outs, axis=2)
    return jnp.transpose(o, (0, 2, 1, 3)).reshape(b, s, h * dh)


def setup_inputs(seed: int = 0) -> dict:
    key = jax.random.key(seed)
    ks = jax.random.split(key, 16)
    f32 = jnp.float32

    def w(k, shape, fan_in):
        return jax.random.normal(k, shape, f32) * (fan_in ** -0.5)

    def gain(k, shape):
        return 1.0 + 0.02 * jax.random.normal(k, shape, f32)

    return {
        "x": jax.random.normal(ks[0], (BATCH, SEQ, D_MODEL), f32),
        "ffn1_norm": gain(ks[1], (DEPTH, D_MODEL)),
        "ffn1_w13": w(ks[2], (DEPTH, D_MODEL, 2 * D_FF), D_MODEL),
        "ffn1_w2": w(ks[3], (DEPTH, D_FF, D_MODEL), D_FF),
        "mix_norm": gain(ks[4], (DEPTH, D_MODEL)),
        "w_in": w(ks[5], (DEPTH, D_MODEL, IN_COLS), D_MODEL),
        "b_gate": 0.01 * jax.random.normal(ks[6], (DEPTH, N_BRANCH * D_MODEL), f32),
        "conv_w": w(ks[7], (DEPTH, CONV_K, CONV_WIDTH), CONV_K),
        "w_conv_o": w(ks[8], (DEPTH, CONV_WIDTH, D_MODEL), CONV_WIDTH),
        "w_attn_o": w(ks[9], (DEPTH, ATTN_WIDTH, D_MODEL), ATTN_WIDTH),
        "w_out": w(ks[10], (DEPTH, D_MODEL, D_MODEL), D_MODEL),
        "ffn2_norm": gain(ks[11], (DEPTH, D_MODEL)),
        "ffn2_w13": w(ks[12], (DEPTH, D_MODEL, 2 * D_FF), D_MODEL),
        "ffn2_w2": w(ks[13], (DEPTH, D_FF, D_MODEL), D_FF),
        "final_norm": gain(ks[14], (D_MODEL,)),
    }


def reference(x, ffn1_norm, ffn1_w13, ffn1_w2, mix_norm, w_in, b_gate, conv_w,
              w_conv_o, w_attn_o, w_out, ffn2_norm, ffn2_w13, ffn2_w2, final_norm):
    b, s, _ = x.shape
    splits = [CONV_WIDTH, 2 * CONV_WIDTH, 3 * CONV_WIDTH,
              3 * CONV_WIDTH + ATTN_WIDTH, 3 * CONV_WIDTH + 2 * ATTN_WIDTH,
              3 * CONV_WIDTH + 3 * ATTN_WIDTH]
    for l in range(DEPTH):
        x = x + 0.5 * swiglu(rmsnorm(x, ffn1_norm[l]), ffn1_w13[l], ffn1_w2[l])

        h = rmsnorm(x, mix_norm[l])
        proj = h @ w_in[l]
        cb, cc, cx, q, k, v, gl = jnp.split(proj, splits, axis=-1)
        gates = jax.nn.sigmoid(gl + b_gate[l])
        g_conv, g_attn = jnp.split(gates, 2, axis=-1)

        conv_y = cb * causal_depthwise_conv(cc * cx, conv_w[l])
        conv_branch = conv_y @ w_conv_o[l]

        attn_y = stick_breaking_attention(q.reshape(b, s, N_HEADS, HEAD_DIM),
                                          k.reshape(b, s, N_HEADS, HEAD_DIM),
                                          v.reshape(b, s, N_HEADS, HEAD_DIM))
        attn_branch = attn_y @ w_attn_o[l]

        merged = g_conv * conv_branch + g_attn * attn_branch
        x = x + merged @ w_out[l]

        x = x + 0.5 * swiglu(rmsnorm(x, ffn2_norm[l]), ffn2_w13[l], ffn2_w2[l])
    return rmsnorm(x, final_norm)
```

```python
import functools
import math

import jax
import jax.numpy as jnp
from jax import lax
from jax.experimental import pallas as pl
from jax.experimental.pallas import tpu as pltpu

F32 = jnp.float32
BF16 = jnp.bfloat16

RMS_EPS = 1e-6
HEAD_DIM = 64
CONV_K = 3
LANES = 128
SUBLANES = 8
MIB = 1024 * 1024


def _resident(shape):
    return pl.BlockSpec(shape, lambda *_: (0,) * len(shape), pipeline_mode=pl.Buffered(1))


def _rmsnorm(x, g):
    ms = jnp.mean(x * x, axis=-1, keepdims=True)
    return x * lax.rsqrt(ms + RMS_EPS) * g


def _dot(a, b):
    return jnp.dot(a, b, preferred_element_type=F32)


def _ffn_kernel(x_ref, g_ref, w13_ref, w2_ref, gf_ref, o_ref, act_ref, *, d_ff, ff_chunk, final_norm):
    x = x_ref[...]
    h = _rmsnorm(x, g_ref[...]).astype(BF16)
    for c in range(0, d_ff, ff_chunk):
        a = _dot(h, w13_ref[:, c:c + ff_chunk])
        b = _dot(h, w13_ref[:, d_ff + c:d_ff + c + ff_chunk])
        act_ref[:, c:c + ff_chunk] = (a * jax.nn.sigmoid(a) * b).astype(BF16)
    y = x + 0.5 * _dot(act_ref[...], w2_ref[...])
    if final_norm:
        y = _rmsnorm(y, gf_ref[...])
    o_ref[...] = y


def _ffn(x, g, w13, w2, gf, *, final_norm, tm=512, ff_chunk=256):
    n, d = x.shape
    d_ff = w2.shape[0]
    assert n % tm == 0 and d_ff % ff_chunk == 0 and w13.shape == (d, 2 * d_ff)
    row = pl.BlockSpec((tm, d), lambda i: (i, 0))
    return pl.pallas_call(
        functools.partial(_ffn_kernel, d_ff=d_ff, ff_chunk=ff_chunk, final_norm=final_norm),
        grid=(n // tm,),
        in_specs=[row, _resident((1, d)), _resident((d, 2 * d_ff)), _resident((d_ff, d)), _resident((1, d))],
        out_specs=row,
        out_shape=jax.ShapeDtypeStruct((n, d), F32),
        scratch_shapes=[pltpu.VMEM((tm, d_ff), BF16)],
        compiler_params=pltpu.CompilerParams(
            dimension_semantics=("parallel",), vmem_limit_bytes=52 * MIB),
        name="ffn",
    )(x, g, w13, w2, gf)


def _qkv_kernel(x_ref, g_ref, w_ref, q_ref, k_ref, v_ref, *, width, scale):
    h = _rmsnorm(x_ref[...], g_ref[...]).astype(BF16)
    q_ref[...] = (_dot(h, w_ref[:, 0:width]) * scale).astype(BF16)
    k_ref[...] = _dot(h, w_ref[:, width:2 * width]).astype(BF16)
    v_ref[...] = _dot(h, w_ref[:, 2 * width:3 * width]).astype(BF16)


def _qkv(x, g, w_qkv, *, tm=512):
    n, d = x.shape
    width = w_qkv.shape[1] // 3
    assert n % tm == 0
    out = jax.ShapeDtypeStruct((n, width), BF16)
    out_spec = pl.BlockSpec((tm, width), lambda i: (i, 0))
    return pl.pallas_call(
        functools.partial(_qkv_kernel, width=width, scale=1.0 / math.sqrt(HEAD_DIM)),
        grid=(n // tm,),
        in_specs=[pl.BlockSpec((tm, d), lambda i: (i, 0)), _resident((1, d)), _resident((d, 3 * width))],
        out_specs=[out_spec, out_spec, out_spec],
        out_shape=[out, out, out],
        compiler_params=pltpu.CompilerParams(
            dimension_semantics=("parallel",), vmem_limit_bytes=32 * MIB),
        name="qkv",
    )(x, g, w_qkv)


def _attn_kernel(q_ref, k_ref, v_ref, w_ref, o_ref, carry_ref, acc_ref, *, tq):
    i = pl.program_id(2)
    q = q_ref[0]
    first_head = lax.broadcasted_iota(jnp.int32, (tq, LANES), 1) < HEAD_DIM
    zeros = jnp.zeros((tq, LANES), BF16)
    qm = jnp.concatenate([jnp.where(first_head, q, zeros), jnp.where(first_head, zeros, q)], axis=0)
    w = w_ref[...]
    row = lax.broadcasted_iota(jnp.int32, (2 * tq, tq), 0) & (tq - 1)
    col = lax.broadcasted_iota(jnp.int32, (2 * tq, tq), 1)
    causal = col < row

    def block(c, diagonal):
        start = pl.multiple_of(c * tq, tq)
        kc = k_ref[0, pl.ds(start, tq), :]
        vc = v_ref[0, pl.ds(start, tq), :]
        z = lax.dot_general(qm, kc, (((1,), (1,)), ((), ())), preferred_element_type=F32)
        sp = jnp.log(1.0 + jnp.exp(-jnp.abs(z)))
        mz = jnp.minimum(z, 0.0)
        log_beta = mz - sp
        log_keep = (mz - z) - sp
        if diagonal:
            log_keep = jnp.where(causal, log_keep, 0.0)
        hi = log_keep.astype(BF16)
        lo = (log_keep - hi.astype(F32)).astype(BF16)
        sums = _dot(jnp.concatenate([hi, lo], axis=1), w)
        carry = carry_ref[...]
        a = jnp.exp(log_beta + (carry + sums[:, :tq]))
        if diagonal:
            a = jnp.where(causal, a, 0.0)
        carry_ref[...] = carry + sums[:, tq:]
        ab = a.astype(BF16)
        a_pair = jnp.concatenate([ab[:tq], ab[tq:]], axis=1)
        v_pair = jnp.concatenate([jnp.where(first_head, vc, zeros), jnp.where(first_head, zeros, vc)], axis=0)
        acc_ref[...] += _dot(a_pair, v_pair)

    carry_ref[...] = jnp.zeros_like(carry_ref)
    acc_ref[...] = jnp.zeros_like(acc_ref)
    block(i, True)

    def body(step, _):
        block(i - 1 - step, False)
        return 0

    lax.fori_loop(0, i, body, 0)
    o_ref[0] = acc_ref[...].astype(BF16)


def _suffix_sum_matrix(tq):
    j = jnp.arange(2 * tq)[:, None] % tq
    s = jnp.arange(2 * tq)[None, :]
    return jnp.where((s >= tq) | (j > s), 1.0, 0.0).astype(BF16)


def _attn(q, k, v, *, tq=LANES):
    b, s, width = q.shape
    pairs = width // LANES
    assert tq == LANES and s % tq == 0 and width % LANES == 0
    q_spec = pl.BlockSpec((1, tq, LANES), lambda bi, p, i: (bi, i, p))
    kv_spec = pl.BlockSpec((1, s, LANES), lambda bi, p, i: (bi, 0, p))
    return pl.pallas_call(
        functools.partial(_attn_kernel, tq=tq),
        grid=(b, pairs, s // tq),
        in_specs=[q_spec, kv_spec, kv_spec, _resident((2 * tq, 2 * tq))],
        out_specs=q_spec,
        out_shape=jax.ShapeDtypeStruct((b, s, width), BF16),
        scratch_shapes=[pltpu.VMEM((2 * tq, tq), F32), pltpu.VMEM((tq, LANES), F32)],
        compiler_params=pltpu.CompilerParams(
            dimension_semantics=("parallel", "parallel", "parallel"), vmem_limit_bytes=32 * MIB),
        name="attn",
    )(q, k, v, _suffix_sum_matrix(tq))


def _shift_rows(u, prev, shift):
    rolled = pltpu.roll(u, shift, axis=0)
    tail = pltpu.roll(prev, shift, axis=0)
    first = lax.broadcasted_iota(jnp.int32, tail.shape, 0) < shift
    head = jnp.where(first, tail, rolled[:SUBLANES])
    return jnp.concatenate([head, rolled[SUBLANES:]], axis=0)


def _mix_out_kernel(x_ref, y_ref, g_ref, wc_ref, wg_ref, bg_ref, cw_ref, wco_ref, wao_ref, wo_ref,
                    o_ref, tail_ref, *, cw_width):
    @pl.when(pl.program_id(1) == 0)
    def _():
        tail_ref[...] = jnp.zeros_like(tail_ref)

    x = x_ref[...]
    d = x.shape[-1]
    h = _rmsnorm(x, g_ref[...]).astype(BF16)
    cb = _dot(h, wc_ref[:, 0:cw_width])
    cc = _dot(h, wc_ref[:, cw_width:2 * cw_width])
    cx = _dot(h, wc_ref[:, 2 * cw_width:3 * cw_width])
    u = cc * cx
    prev = tail_ref[...]
    tail_ref[...] = u[u.shape[0] - SUBLANES:]
    cw = cw_ref[...]
    conv = cw[0:1] * _shift_rows(u, prev, 2) + cw[1:2] * _shift_rows(u, prev, 1) + cw[2:3] * u
    conv_branch = _dot((cb * conv).astype(BF16), wco_ref[...])
    attn_branch = _dot(y_ref[...], wao_ref[...])
    gates = jax.nn.sigmoid(_dot(h, wg_ref[...]) + bg_ref[...])
    merged = gates[:, :d] * conv_branch + gates[:, d:] * attn_branch
    o_ref[...] = x + _dot(merged.astype(BF16), wo_ref[...])


def _mix_out(x, attn_y, g, w_conv_in, w_gate, b_gate, conv_w, w_conv_o, w_attn_o, w_out, *, seq, tm=256):
    n, d = x.shape
    cw_width = w_conv_o.shape[0]
    aw = attn_y.shape[1]
    assert seq % tm == 0 and n % seq == 0 and conv_w.shape == (CONV_K, cw_width)
    tiles = seq // tm
    row = lambda width: pl.BlockSpec((tm, width), lambda bi, j: (bi * tiles + j, 0))
    return pl.pallas_call(
        functools.partial(_mix_out_kernel, cw_width=cw_width),
        grid=(n // seq, tiles),
        in_specs=[row(d), row(aw), _resident((1, d)), _resident((d, 3 * cw_width)), _resident((d, 2 * d)),
                  _resident((1, 2 * d)), _resident((CONV_K, cw_width)), _resident((cw_width, d)),
                  _resident((aw, d)), _resident((d, d))],
        out_specs=row(d),
        out_shape=jax.ShapeDtypeStruct((n, d), F32),
        scratch_shapes=[pltpu.VMEM((SUBLANES, cw_width), F32)],
        compiler_params=pltpu.CompilerParams(
            dimension_semantics=("arbitrary", "arbitrary"), vmem_limit_bytes=48 * MIB),
        name="mix_out",
    )(x, attn_y, g, w_conv_in, w_gate, b_gate, conv_w, w_conv_o, w_attn_o, w_out)


def kernel(x, ffn1_norm, ffn1_w13, ffn1_w2, mix_norm, w_in, b_gate, conv_w, w_conv_o, w_attn_o, w_out,
           ffn2_norm, ffn2_w13, ffn2_w2, final_norm):
    b, s, d = x.shape
    depth = w_in.shape[0]
    cw_width = w_conv_o.shape[1]
    aw = w_attn_o.shape[1]
    qkv_lo, qkv_hi = 3 * cw_width, 3 * cw_width + 3 * aw
    gf = final_norm.reshape(1, d)
    xf = x.reshape(b * s, d)
    for l in range(depth):
        w_in_l = w_in[l].astype(BF16)
        xf = _ffn(xf, ffn1_norm[l].reshape(1, d), ffn1_w13[l].astype(BF16), ffn1_w2[l].astype(BF16), gf,
                  final_norm=False)
        q, k, v = _qkv(xf, mix_norm[l].reshape(1, d), w_in_l[:, qkv_lo:qkv_hi])
        attn_y = _attn(q.reshape(b, s, aw), k.reshape(b, s, aw), v.reshape(b, s, aw)).reshape(b * s, aw)
        xf = _mix_out(xf, attn_y, mix_norm[l].reshape(1, d), w_in_l[:, :qkv_lo], w_in_l[:, qkv_hi:],
                      b_gate[l].reshape(1, 2 * d), conv_w[l], w_conv_o[l].astype(BF16),
                      w_attn_o[l].astype(BF16), w_out[l].astype(BF16), seq=s)
        xf = _ffn(xf, ffn2_norm[l].reshape(1, d), ffn2_w13[l].astype(BF16), ffn2_w2[l].astype(BF16), gf,
                  final_norm=(l == depth - 1))
    return xf.reshape(b, s, d)
```

```python
import functools
import math

import jax
import jax.numpy as jnp
from jax import lax
from jax.experimental import pallas as pl
from jax.experimental.pallas import tpu as pltpu

F32 = jnp.float32
BF16 = jnp.bfloat16

RMS_EPS = 1e-6
LOG2E = 1.4426950408889634
HEAD_DIM = 64
CONV_K = 3
LANES = 128
SUBLANES = 8
MIB = 1024 * 1024


def _resident(shape):
    return pl.BlockSpec(shape, lambda *_: (0,) * len(shape), pipeline_mode=pl.Buffered(1))


def _rmsnorm(x, g):
    ms = jnp.mean(x * x, axis=-1, keepdims=True)
    return x * lax.rsqrt(ms + RMS_EPS) * g


def _dot(a, b):
    return jnp.dot(a, b, preferred_element_type=F32)


def _ffn_kernel(x_ref, g_ref, w13_ref, w2_ref, gf_ref, o_ref, act_ref, *, d_ff, ff_chunk, final_norm):
    x = x_ref[...]
    h = _rmsnorm(x, g_ref[...]).astype(BF16)
    for c in range(0, d_ff, ff_chunk):
        a = _dot(h, w13_ref[:, c:c + ff_chunk])
        b = _dot(h, w13_ref[:, d_ff + c:d_ff + c + ff_chunk])
        act_ref[:, c:c + ff_chunk] = (a * jax.nn.sigmoid(a) * b).astype(BF16)
    y = x + 0.5 * _dot(act_ref[...], w2_ref[...])
    if final_norm:
        y = _rmsnorm(y, gf_ref[...])
    o_ref[...] = y


def _ffn(x, g, w13, w2, gf, *, final_norm, tm=512, ff_chunk=256):
    n, d = x.shape
    d_ff = w2.shape[0]
    assert n % tm == 0 and d_ff % ff_chunk == 0 and w13.shape == (d, 2 * d_ff)
    row = pl.BlockSpec((tm, d), lambda i: (i, 0))
    return pl.pallas_call(
        functools.partial(_ffn_kernel, d_ff=d_ff, ff_chunk=ff_chunk, final_norm=final_norm),
        grid=(n // tm,),
        in_specs=[row, _resident((1, d)), _resident((d, 2 * d_ff)), _resident((d_ff, d)), _resident((1, d))],
        out_specs=row,
        out_shape=jax.ShapeDtypeStruct((n, d), F32),
        scratch_shapes=[pltpu.VMEM((tm, d_ff), BF16)],
        compiler_params=pltpu.CompilerParams(
            dimension_semantics=("parallel",), vmem_limit_bytes=52 * MIB),
        name="ffn",
    )(x, g, w13, w2, gf)


def _qkv_kernel(x_ref, g_ref, w_ref, q_ref, k_ref, v_ref, *, width, scale):
    h = _rmsnorm(x_ref[...], g_ref[...]).astype(BF16)
    q_ref[...] = (_dot(h, w_ref[:, 0:width]) * scale).astype(BF16)
    k_ref[...] = _dot(h, w_ref[:, width:2 * width]).astype(BF16)
    v_ref[...] = _dot(h, w_ref[:, 2 * width:3 * width]).astype(BF16)


def _qkv(x, g, w_qkv, *, tm=512):
    n, d = x.shape
    width = w_qkv.shape[1] // 3
    assert n % tm == 0
    out = jax.ShapeDtypeStruct((n, width), BF16)
    out_spec = pl.BlockSpec((tm, width), lambda i: (i, 0))
    return pl.pallas_call(
        functools.partial(_qkv_kernel, width=width, scale=LOG2E / math.sqrt(HEAD_DIM)),
        grid=(n // tm,),
        in_specs=[pl.BlockSpec((tm, d), lambda i: (i, 0)), _resident((1, d)), _resident((d, 3 * width))],
        out_specs=[out_spec, out_spec, out_spec],
        out_shape=[out, out, out],
        compiler_params=pltpu.CompilerParams(
            dimension_semantics=("parallel",), vmem_limit_bytes=32 * MIB),
        name="qkv",
    )(x, g, w_qkv)


def _attn_kernel(q_ref, k_ref, v_ref, w_ref, o_ref, qm_ref, carry_ref, acc_ref, *, tq, pairs):
    i = pl.program_id(1)
    first_head = lax.broadcasted_iota(jnp.int32, (tq, LANES), 1) < HEAD_DIM
    zeros = jnp.zeros((tq, LANES), BF16)

    def split_heads(t):
        return jnp.concatenate([jnp.where(first_head, t, zeros), jnp.where(first_head, zeros, t)], axis=0)

    row = lax.broadcasted_iota(jnp.int32, (2 * tq, tq), 0) & (tq - 1)
    col = lax.broadcasted_iota(jnp.int32, (2 * tq, tq), 1)
    causal = col < row

    def block(c, diagonal):
        start = pl.multiple_of(c * tq, tq)
        lanes = [slice(p * LANES, (p + 1) * LANES) for p in range(pairs)]
        zs = [lax.dot_general(qm_ref[p], k_ref[0, pl.ds(start, tq), ln],
                              (((1,), (1,)), ((), ())), preferred_element_type=F32)
              for p, ln in enumerate(lanes)]
        log_betas, sums = [], []
        for z in zs:
            neg_part = jnp.minimum(z, 0.0)
            neg_relu = neg_part - z
            sp = jnp.log2(1.0 + jnp.exp2(neg_part + neg_relu))
            log_beta = neg_part - sp
            log_keep = neg_relu - sp
            if diagonal:
                log_keep = jnp.where(causal, log_keep, 0.0)
            hi = log_keep.astype(BF16)
            lo = (log_keep - hi.astype(F32)).astype(BF16)
            log_betas.append(log_beta)
            sums.append(_dot(jnp.concatenate([hi, lo], axis=1), w_ref[...]))
        for p, ln in enumerate(lanes):
            carry = carry_ref[p]
            a = jnp.exp2(log_betas[p] + (carry + sums[p][:, :tq]))
            if diagonal:
                a = jnp.where(causal, a, 0.0)
            carry_ref[p] = carry + sums[p][:, tq:]
            ab = a.astype(BF16)
            a_pair = jnp.concatenate([ab[:tq], ab[tq:]], axis=1)
            acc_ref[:, ln] += _dot(a_pair, split_heads(v_ref[0, pl.ds(start, tq), ln]))

    for p in range(pairs):
        qm_ref[p] = split_heads(q_ref[0, :, p * LANES:(p + 1) * LANES])
    carry_ref[...] = jnp.zeros_like(carry_ref)
    acc_ref[...] = jnp.zeros_like(acc_ref)
    block(i, True)

    def body(step, _):
        block(i - 1 - step, False)
        return 0

    lax.fori_loop(0, i, body, 0)
    o_ref[0] = acc_ref[...].astype(BF16)


def _suffix_sum_matrix(tq):
    j = jnp.arange(2 * tq)[:, None] % tq
    s = jnp.arange(2 * tq)[None, :]
    return jnp.where((s >= tq) | (j > s), 1.0, 0.0).astype(BF16)


def _attn(q, k, v, *, tq=LANES):
    b, s, width = q.shape
    pairs = width // LANES
    assert tq == LANES and s % tq == 0 and width % LANES == 0
    q_spec = pl.BlockSpec((1, tq, width), lambda bi, i: (bi, i, 0))
    kv_spec = pl.BlockSpec((1, s, width), lambda bi, i: (bi, 0, 0))
    return pl.pallas_call(
        functools.partial(_attn_kernel, tq=tq, pairs=pairs),
        grid=(b, s // tq),
        in_specs=[q_spec, kv_spec, kv_spec, _resident((2 * tq, 2 * tq))],
        out_specs=q_spec,
        out_shape=jax.ShapeDtypeStruct((b, s, width), BF16),
        scratch_shapes=[pltpu.VMEM((pairs, 2 * tq, LANES), BF16), pltpu.VMEM((pairs, 2 * tq, tq), F32),
                        pltpu.VMEM((tq, width), F32)],
        compiler_params=pltpu.CompilerParams(
            dimension_semantics=("parallel", "parallel"), vmem_limit_bytes=32 * MIB),
        name="attn",
    )(q, k, v, _suffix_sum_matrix(tq))


def _shift_rows(u, prev, shift):
    rolled = pltpu.roll(u, shift, axis=0)
    tail = pltpu.roll(prev, shift, axis=0)
    first = lax.broadcasted_iota(jnp.int32, tail.shape, 0) < shift
    head = jnp.where(first, tail, rolled[:SUBLANES])
    return jnp.concatenate([head, rolled[SUBLANES:]], axis=0)


def _mix_out_kernel(x_ref, y_ref, g_ref, wc_ref, wg_ref, bg_ref, cw_ref, wco_ref, wao_ref, wo_ref,
                    o_ref, tail_ref, *, cw_width):
    @pl.when(pl.program_id(1) == 0)
    def _():
        tail_ref[...] = jnp.zeros_like(tail_ref)

    x = x_ref[...]
    d = x.shape[-1]
    h = _rmsnorm(x, g_ref[...]).astype(BF16)
    cb = _dot(h, wc_ref[:, 0:cw_width])
    cc = _dot(h, wc_ref[:, cw_width:2 * cw_width])
    cx = _dot(h, wc_ref[:, 2 * cw_width:3 * cw_width])
    u = cc * cx
    prev = tail_ref[...]
    tail_ref[...] = u[u.shape[0] - SUBLANES:]
    cw = cw_ref[...]
    conv = cw[0:1] * _shift_rows(u, prev, 2) + cw[1:2] * _shift_rows(u, prev, 1) + cw[2:3] * u
    conv_branch = _dot((cb * conv).astype(BF16), wco_ref[...])
    attn_branch = _dot(y_ref[...], wao_ref[...])
    gates = jax.nn.sigmoid(_dot(h, wg_ref[...]) + bg_ref[...])
    merged = gates[:, :d] * conv_branch + gates[:, d:] * attn_branch
    o_ref[...] = x + _dot(merged.astype(BF16), wo_ref[...])


def _mix_out(x, attn_y, g, w_conv_in, w_gate, b_gate, conv_w, w_conv_o, w_attn_o, w_out, *, seq, tm=256):
    n, d = x.shape
    cw_width = w_conv_o.shape[0]
    aw = attn_y.shape[1]
    assert seq % tm == 0 and n % seq == 0 and conv_w.shape == (CONV_K, cw_width)
    tiles = seq // tm
    row = lambda width: pl.BlockSpec((tm, width), lambda bi, j: (bi * tiles + j, 0))
    return pl.pallas_call(
        functools.partial(_mix_out_kernel, cw_width=cw_width),
        grid=(n // seq, tiles),
        in_specs=[row(d), row(aw), _resident((1, d)), _resident((d, 3 * cw_width)), _resident((d, 2 * d)),
                  _resident((1, 2 * d)), _resident((CONV_K, cw_width)), _resident((cw_width, d)),
                  _resident((aw, d)), _resident((d, d))],
        out_specs=row(d),
        out_shape=jax.ShapeDtypeStruct((n, d), F32),
        scratch_shapes=[pltpu.VMEM((SUBLANES, cw_width), F32)],
        compiler_params=pltpu.CompilerParams(
            dimension_semantics=("arbitrary", "arbitrary"), vmem_limit_bytes=48 * MIB),
        name="mix_out",
    )(x, attn_y, g, w_conv_in, w_gate, b_gate, conv_w, w_conv_o, w_attn_o, w_out)


def kernel(x, ffn1_norm, ffn1_w13, ffn1_w2, mix_norm, w_in, b_gate, conv_w, w_conv_o, w_attn_o, w_out,
           ffn2_norm, ffn2_w13, ffn2_w2, final_norm):
    b, s, d = x.shape
    depth = w_in.shape[0]
    cw_width = w_conv_o.shape[1]
    aw = w_attn_o.shape[1]
    qkv_lo, qkv_hi = 3 * cw_width, 3 * cw_width + 3 * aw
    gf = final_norm.reshape(1, d)
    xf = x.reshape(b * s, d)
    for l in range(depth):
        w_in_l = w_in[l].astype(BF16)
        xf = _ffn(xf, ffn1_norm[l].reshape(1, d), ffn1_w13[l].astype(BF16), ffn1_w2[l].astype(BF16), gf,
                  final_norm=False)
        q, k, v = _qkv(xf, mix_norm[l].reshape(1, d), w_in_l[:, qkv_lo:qkv_hi])
        attn_y = _attn(q.reshape(b, s, aw), k.reshape(b, s, aw), v.reshape(b, s, aw)).reshape(b * s, aw)
        xf = _mix_out(xf, attn_y, mix_norm[l].reshape(1, d), w_in_l[:, :qkv_lo], w_in_l[:, qkv_hi:],
                      b_gate[l].reshape(1, 2 * d), conv_w[l], w_conv_o[l].astype(BF16),
                      w_attn_o[l].astype(BF16), w_out[l].astype(BF16), seq=s)
        xf = _ffn(xf, ffn2_norm[l].reshape(1, d), ffn2_w13[l].astype(BF16), ffn2_w2[l].astype(BF16), gf,
                  final_norm=(l == depth - 1))
    return xf.reshape(b, s, d)
```

```python
import functools
import math

import jax
import jax.numpy as jnp
from jax import lax
from jax.experimental import pallas as pl
from jax.experimental.pallas import tpu as pltpu

F32 = jnp.float32
BF16 = jnp.bfloat16

RMS_EPS = 1e-6
LOG2E = 1.4426950408889634
DEAD_LOG2 = -160.0
HEAD_DIM = 64
CONV_K = 3
LANES = 128
SUBLANES = 8
MIB = 1024 * 1024


def _resident(shape, col_block=0):
    index = (0,) * (len(shape) - 1) + (col_block,)
    return pl.BlockSpec(shape, lambda *_: index, pipeline_mode=pl.Buffered(1))


def _rmsnorm(x, g):
    ms = jnp.mean(x * x, axis=-1, keepdims=True)
    return x * lax.rsqrt(ms + RMS_EPS) * g


def _dot(a, b):
    return jnp.dot(a, b, preferred_element_type=F32)


def _ffn_kernel(x_ref, g_ref, w13_ref, w2_ref, gf_ref, o_ref, act_ref, *, d_ff, ff_chunk, final_norm):
    x = x_ref[...]
    h = _rmsnorm(x, g_ref[...]).astype(BF16)
    for c in range(0, d_ff, ff_chunk):
        a = _dot(h, w13_ref[:, c:c + ff_chunk])
        b = _dot(h, w13_ref[:, d_ff + c:d_ff + c + ff_chunk])
        act_ref[:, c:c + ff_chunk] = (a * jax.nn.sigmoid(a) * b).astype(BF16)
    y = x + 0.5 * _dot(act_ref[...], w2_ref[...])
    if final_norm:
        y = _rmsnorm(y, gf_ref[...])
    o_ref[...] = y


def _ffn(x, g, w13, w2, gf, *, final_norm, tm=512, ff_chunk=256):
    n, d = x.shape
    d_ff = w2.shape[0]
    assert n % tm == 0 and d_ff % ff_chunk == 0 and w13.shape == (d, 2 * d_ff)
    row = pl.BlockSpec((tm, d), lambda i: (i, 0))
    return pl.pallas_call(
        functools.partial(_ffn_kernel, d_ff=d_ff, ff_chunk=ff_chunk, final_norm=final_norm),
        grid=(n // tm,),
        in_specs=[row, _resident((1, d)), _resident((d, 2 * d_ff)), _resident((d_ff, d)), _resident((1, d))],
        out_specs=row,
        out_shape=jax.ShapeDtypeStruct((n, d), F32),
        scratch_shapes=[pltpu.VMEM((tm, d_ff), BF16)],
        compiler_params=pltpu.CompilerParams(
            dimension_semantics=("parallel",), vmem_limit_bytes=52 * MIB),
        name="ffn",
    )(x, g, w13, w2, gf)


def _qkv_kernel(x_ref, g_ref, w_ref, q_ref, k_ref, v_ref, *, width, scale):
    h = _rmsnorm(x_ref[...], g_ref[...]).astype(BF16)
    q_ref[...] = (_dot(h, w_ref[:, 0:width]) * scale).astype(BF16)
    k_ref[...] = _dot(h, w_ref[:, width:2 * width]).astype(BF16)
    v_ref[...] = _dot(h, w_ref[:, 2 * width:3 * width]).astype(BF16)


def _qkv(x, g, w_in, *, width, col_block, tm=512):
    n, d = x.shape
    assert n % tm == 0
    out = jax.ShapeDtypeStruct((n, width), BF16)
    out_spec = pl.BlockSpec((tm, width), lambda i: (i, 0))
    return pl.pallas_call(
        functools.partial(_qkv_kernel, width=width, scale=LOG2E / math.sqrt(HEAD_DIM)),
        grid=(n // tm,),
        in_specs=[pl.BlockSpec((tm, d), lambda i: (i, 0)), _resident((1, d)),
                  _resident((d, 3 * width), col_block)],
        out_specs=[out_spec, out_spec, out_spec],
        out_shape=[out, out, out],
        compiler_params=pltpu.CompilerParams(
            dimension_semantics=("parallel",), vmem_limit_bytes=32 * MIB),
        name="qkv",
    )(x, g, w_in)


def _attn_kernel(q_ref, k_ref, v_ref, w_ref, o_ref, qm_ref, carry_ref, acc_ref, *, tq, pairs):
    i = pl.program_id(1)
    first_head = lax.broadcasted_iota(jnp.int32, (tq, LANES), 1) < HEAD_DIM
    zeros = jnp.zeros((tq, LANES), BF16)

    def split_heads(t):
        return jnp.concatenate([jnp.where(first_head, t, zeros), jnp.where(first_head, zeros, t)], axis=0)

    row = lax.broadcasted_iota(jnp.int32, (2 * tq, tq), 0) & (tq - 1)
    col = lax.broadcasted_iota(jnp.int32, (2 * tq, tq), 1)
    causal = col < row

    def block(c, diagonal):
        start = pl.multiple_of(c * tq, tq)
        lanes = [slice(p * LANES, (p + 1) * LANES) for p in range(pairs)]
        zs = [lax.dot_general(qm_ref[p], k_ref[0, pl.ds(start, tq), ln],
                              (((1,), (1,)), ((), ())), preferred_element_type=F32)
              for p, ln in enumerate(lanes)]
        log_betas, sums = [], []
        for z in zs:
            neg_part = jnp.minimum(z, 0.0)
            neg_relu = neg_part - z
            sp = jnp.log2(1.0 + jnp.exp2(neg_part + neg_relu))
            log_beta = neg_part - sp
            log_keep = neg_relu - sp
            if diagonal:
                log_keep = jnp.where(causal, log_keep, 0.0)
            hi = log_keep.astype(BF16)
            lo = (log_keep - hi.astype(F32)).astype(BF16)
            log_betas.append(log_beta)
            sums.append(_dot(jnp.concatenate([hi, lo], axis=1), w_ref[...]))
        for p, ln in enumerate(lanes):
            carry = carry_ref[p]
            a = jnp.exp2(log_betas[p] + (carry + sums[p][:, :tq]))
            if diagonal:
                a = jnp.where(causal, a, 0.0)
            carry_ref[p] = carry + sums[p][:, tq:]
            ab = a.astype(BF16)
            a_pair = jnp.concatenate([ab[:tq], ab[tq:]], axis=1)
            acc_ref[:, ln] += _dot(a_pair, split_heads(v_ref[0, pl.ds(start, tq), ln]))

    for p in range(pairs):
        qm_ref[p] = split_heads(q_ref[0, :, p * LANES:(p + 1) * LANES])
    carry_ref[...] = jnp.zeros_like(carry_ref)
    acc_ref[...] = jnp.zeros_like(acc_ref)
    block(i, True)

    def any_row_alive():
        return (jnp.max(jnp.max(carry_ref[...], axis=0)) > DEAD_LOG2).astype(jnp.int32)

    def body(state):
        c, _ = state
        block(c, False)
        return c - 1, any_row_alive()

    lax.while_loop(lambda state: (state[0] >= 0) & (state[1] > 0), body, (i - 1, any_row_alive()))
    o_ref[0] = acc_ref[...].astype(BF16)


def _suffix_sum_matrix(tq):
    j = jnp.arange(2 * tq)[:, None] % tq
    s = jnp.arange(2 * tq)[None, :]
    return jnp.where((s >= tq) | (j > s), 1.0, 0.0).astype(BF16)


def _attn(q, k, v, *, tq=LANES):
    b, s, width = q.shape
    pairs = width // LANES
    assert tq == LANES and s % tq == 0 and width % LANES == 0
    q_spec = pl.BlockSpec((1, tq, width), lambda bi, i: (bi, i, 0))
    kv_spec = pl.BlockSpec((1, s, width), lambda bi, i: (bi, 0, 0))
    return pl.pallas_call(
        functools.partial(_attn_kernel, tq=tq, pairs=pairs),
        grid=(b, s // tq),
        in_specs=[q_spec, kv_spec, kv_spec, _resident((2 * tq, 2 * tq))],
        out_specs=q_spec,
        out_shape=jax.ShapeDtypeStruct((b, s, width), BF16),
        scratch_shapes=[pltpu.VMEM((pairs, 2 * tq, LANES), BF16), pltpu.VMEM((pairs, 2 * tq, tq), F32),
                        pltpu.VMEM((tq, width), F32)],
        compiler_params=pltpu.CompilerParams(
            dimension_semantics=("parallel", "parallel"), vmem_limit_bytes=32 * MIB),
        name="attn",
    )(q, k, v, _suffix_sum_matrix(tq))


def _shift_rows(u, prev, shift):
    rolled = pltpu.roll(u, shift, axis=0)
    tail = pltpu.roll(prev, shift, axis=0)
    first = lax.broadcasted_iota(jnp.int32, tail.shape, 0) < shift
    head = jnp.where(first, tail, rolled[:SUBLANES])
    return jnp.concatenate([head, rolled[SUBLANES:]], axis=0)


def _mix_out_kernel(x_ref, y_ref, g_ref, wc_ref, wgc_ref, wga_ref, bgc_ref, bga_ref, cw_ref, wco_ref, wao_ref,
                    wo_ref, o_ref, tail_ref, *, cw_width):
    @pl.when(pl.program_id(1) == 0)
    def _():
        tail_ref[...] = jnp.zeros_like(tail_ref)

    x = x_ref[...]
    h = _rmsnorm(x, g_ref[...]).astype(BF16)
    cb = _dot(h, wc_ref[:, 0:cw_width])
    cc = _dot(h, wc_ref[:, cw_width:2 * cw_width])
    cx = _dot(h, wc_ref[:, 2 * cw_width:3 * cw_width])
    u = cc * cx
    prev = tail_ref[...]
    tail_ref[...] = u[u.shape[0] - SUBLANES:]
    cw = cw_ref[...]
    conv = cw[0:1] * _shift_rows(u, prev, 2) + cw[1:2] * _shift_rows(u, prev, 1) + cw[2:3] * u
    conv_branch = _dot((cb * conv).astype(BF16), wco_ref[...])
    attn_branch = _dot(y_ref[...], wao_ref[...])
    g_conv = jax.nn.sigmoid(_dot(h, wgc_ref[...]) + bgc_ref[...])
    g_attn = jax.nn.sigmoid(_dot(h, wga_ref[...]) + bga_ref[...])
    merged = g_conv * conv_branch + g_attn * attn_branch
    o_ref[...] = x + _dot(merged.astype(BF16), wo_ref[...])


def _mix_out(x, attn_y, g, w_in, b_gate, conv_w, w_conv_o, w_attn_o, w_out, *, seq, tm=256):
    n, d = x.shape
    cw_width = w_conv_o.shape[0]
    aw = attn_y.shape[1]
    gate_col = (3 * cw_width + 3 * aw) // d
    assert seq % tm == 0 and n % seq == 0 and conv_w.shape == (CONV_K, cw_width)
    assert w_in.shape == (d, (gate_col + 2) * d)
    tiles = seq // tm
    row = lambda width: pl.BlockSpec((tm, width), lambda bi, j: (bi * tiles + j, 0))
    return pl.pallas_call(
        functools.partial(_mix_out_kernel, cw_width=cw_width),
        grid=(n // seq, tiles),
        in_specs=[row(d), row(aw), _resident((1, d)), _resident((d, 3 * cw_width)),
                  _resident((d, d), gate_col), _resident((d, d), gate_col + 1),
                  _resident((1, d)), _resident((1, d), 1), _resident((CONV_K, cw_width)),
                  _resident((cw_width, d)), _resident((aw, d)), _resident((d, d))],
        out_specs=row(d),
        out_shape=jax.ShapeDtypeStruct((n, d), F32),
        scratch_shapes=[pltpu.VMEM((SUBLANES, cw_width), F32)],
        compiler_params=pltpu.CompilerParams(
            dimension_semantics=("arbitrary", "arbitrary"), vmem_limit_bytes=48 * MIB),
        name="mix_out",
    )(x, attn_y, g, w_in, w_in, w_in, b_gate, b_gate, conv_w, w_conv_o, w_attn_o, w_out)


def kernel(x, ffn1_norm, ffn1_w13, ffn1_w2, mix_norm, w_in, b_gate, conv_w, w_conv_o, w_attn_o, w_out,
           ffn2_norm, ffn2_w13, ffn2_w2, final_norm):
    b, s, d = x.shape
    depth = w_in.shape[0]
    cw_width = w_conv_o.shape[1]
    aw = w_attn_o.shape[1]
    assert cw_width == aw and (3 * cw_width + 3 * aw) % d == 0
    gf = final_norm.reshape(1, d)
    xf = x.reshape(b * s, d)
    for l in range(depth):
        w_in_l = w_in[l].astype(BF16)
        xf = _ffn(xf, ffn1_norm[l].reshape(1, d), ffn1_w13[l].astype(BF16), ffn1_w2[l].astype(BF16), gf,
                  final_norm=False)
        q, k, v = _qkv(xf, mix_norm[l].reshape(1, d), w_in_l, width=aw, col_block=1)
        attn_y = _attn(q.reshape(b, s, aw), k.reshape(b, s, aw), v.reshape(b, s, aw)).reshape(b * s, aw)
        xf = _mix_out(xf, attn_y, mix_norm[l].reshape(1, d), w_in_l,
                      b_gate[l].reshape(1, 2 * d), conv_w[l], w_conv_o[l].astype(BF16),
                      w_attn_o[l].astype(BF16), w_out[l].astype(BF16), seq=s)
        xf = _ffn(xf, ffn2_norm[l].reshape(1, d), ffn2_w13[l].astype(BF16), ffn2_w2[l].astype(BF16), gf,
                  final_norm=(l == depth - 1))
    return xf.reshape(b, s, d)
```

```python
import functools
import math

import jax
import jax.numpy as jnp
from jax import lax
from jax.experimental import pallas as pl
from jax.experimental.pallas import tpu as pltpu

F32 = jnp.float32
BF16 = jnp.bfloat16

RMS_EPS = 1e-6
LOG2E = 1.4426950408889634
DEAD_LOG2 = -160.0
PARKED_LOG2 = -1e30
HEAD_DIM = 64
CONV_K = 3
LANES = 128
SUBLANES = 8
MIB = 1024 * 1024


def _resident(shape, col_block=0):
    index = (0,) * (len(shape) - 1) + (col_block,)
    return pl.BlockSpec(shape, lambda *_: index, pipeline_mode=pl.Buffered(1))


def _rmsnorm(x, g):
    ms = jnp.mean(x * x, axis=-1, keepdims=True)
    return x * lax.rsqrt(ms + RMS_EPS) * g


def _dot(a, b):
    return jnp.dot(a, b, preferred_element_type=F32)


def _ffn_kernel(x_ref, g_ref, w13_ref, w2_ref, gf_ref, o_ref, act_ref, *, d_ff, ff_chunk, final_norm):
    x = x_ref[...]
    h = _rmsnorm(x, g_ref[...]).astype(BF16)
    for c in range(0, d_ff, ff_chunk):
        a = _dot(h, w13_ref[:, c:c + ff_chunk])
        b = _dot(h, w13_ref[:, d_ff + c:d_ff + c + ff_chunk])
        act_ref[:, c:c + ff_chunk] = (a * jax.nn.sigmoid(a) * b).astype(BF16)
    y = x + 0.5 * _dot(act_ref[...], w2_ref[...])
    if final_norm:
        y = _rmsnorm(y, gf_ref[...])
    o_ref[...] = y


def _ffn(x, g, w13, w2, gf, *, final_norm, tm=512, ff_chunk=256):
    n, d = x.shape
    d_ff = w2.shape[0]
    assert n % tm == 0 and d_ff % ff_chunk == 0 and w13.shape == (d, 2 * d_ff)
    row = pl.BlockSpec((tm, d), lambda i: (i, 0))
    return pl.pallas_call(
        functools.partial(_ffn_kernel, d_ff=d_ff, ff_chunk=ff_chunk, final_norm=final_norm),
        grid=(n // tm,),
        in_specs=[row, _resident((1, d)), _resident((d, 2 * d_ff)), _resident((d_ff, d)), _resident((1, d))],
        out_specs=row,
        out_shape=jax.ShapeDtypeStruct((n, d), F32),
        scratch_shapes=[pltpu.VMEM((tm, d_ff), BF16)],
        compiler_params=pltpu.CompilerParams(
            dimension_semantics=("parallel",), vmem_limit_bytes=52 * MIB),
        name="ffn",
    )(x, g, w13, w2, gf)


def _qkv_kernel(x_ref, g_ref, w_ref, q_ref, k_ref, v_ref, *, width, scale):
    h = _rmsnorm(x_ref[...], g_ref[...]).astype(BF16)
    q_ref[...] = (_dot(h, w_ref[:, 0:width]) * scale).astype(BF16)
    k_ref[...] = _dot(h, w_ref[:, width:2 * width]).astype(BF16)
    v_ref[...] = _dot(h, w_ref[:, 2 * width:3 * width]).astype(BF16)


def _qkv(x, g, w_in, *, width, col_block, tm=1024):
    n, d = x.shape
    assert n % tm == 0
    out = jax.ShapeDtypeStruct((n, width), BF16)
    out_spec = pl.BlockSpec((tm, width), lambda i: (i, 0))
    return pl.pallas_call(
        functools.partial(_qkv_kernel, width=width, scale=LOG2E / math.sqrt(HEAD_DIM)),
        grid=(n // tm,),
        in_specs=[pl.BlockSpec((tm, d), lambda i: (i, 0)), _resident((1, d)),
                  _resident((d, 3 * width), col_block)],
        out_specs=[out_spec, out_spec, out_spec],
        out_shape=[out, out, out],
        compiler_params=pltpu.CompilerParams(
            dimension_semantics=("parallel",), vmem_limit_bytes=32 * MIB),
        name="qkv",
    )(x, g, w_in)


def _attn_kernel(q_ref, k_ref, v_ref, w_ref, o_ref, qm_ref, carry_ref, acc_ref, *, tq, pairs, group):
    i0 = pl.program_id(1) * group
    first_head = lax.broadcasted_iota(jnp.int32, (tq, LANES), 1) < HEAD_DIM
    zeros = jnp.zeros((tq, LANES), BF16)
    lanes = [slice(p * LANES, (p + 1) * LANES) for p in range(pairs)]
    chains = [(g, p) for g in range(group) for p in range(pairs)]

    def split_heads(t):
        return jnp.concatenate([jnp.where(first_head, t, zeros), jnp.where(first_head, zeros, t)], axis=0)

    row = lax.broadcasted_iota(jnp.int32, (2 * tq, tq), 0) & (tq - 1)
    col = lax.broadcasted_iota(jnp.int32, (2 * tq, tq), 1)
    causal = col < row

    def block(j, diagonal):
        starts = [pl.multiple_of(jnp.maximum(i0 + g - j, 0) * tq, tq) for g in range(group)]
        zs = [lax.dot_general(qm_ref[g, p], k_ref[0, pl.ds(starts[g], tq), lanes[p]],
                              (((1,), (1,)), ((), ())), preferred_element_type=F32)
              for g, p in chains]
        log_betas, sums = [], []
        for z in zs:
            neg_part = jnp.minimum(z, 0.0)
            neg_relu = neg_part - z
            sp = jnp.log2(1.0 + jnp.exp2(neg_part + neg_relu))
            log_beta = neg_part - sp
            log_keep = neg_relu - sp
            if diagonal:
                log_keep = jnp.where(causal, log_keep, 0.0)
            hi = log_keep.astype(BF16)
            lo = (log_keep - hi.astype(F32)).astype(BF16)
            log_betas.append(log_beta)
            sums.append(_dot(jnp.concatenate([hi, lo], axis=1), w_ref[...]))
        for n, (g, p) in enumerate(chains):
            carry = carry_ref[g, p]
            a = jnp.exp2(log_betas[n] + (carry + sums[n][:, :tq]))
            if diagonal:
                a = jnp.where(causal, a, 0.0)
            carry_ref[g, p] = carry + sums[n][:, tq:]
            ab = a.astype(BF16)
            a_pair = jnp.concatenate([ab[:tq], ab[tq:]], axis=1)
            acc_ref[g, :, lanes[p]] += _dot(a_pair, split_heads(v_ref[0, pl.ds(starts[g], tq), lanes[p]]))

    for g, p in chains:
        qm_ref[g, p] = split_heads(q_ref[0, g * tq:(g + 1) * tq, lanes[p]])
    carry_ref[...] = jnp.zeros_like(carry_ref)
    acc_ref[...] = jnp.zeros_like(acc_ref)
    block(0, True)

    def any_row_alive():
        live = carry_ref[0, 0]
        for g, p in chains[1:]:
            live = jnp.maximum(live, carry_ref[g, p])
        return (jnp.max(live) > DEAD_LOG2).astype(jnp.int32)

    def body(state):
        j, _ = state
        for g in range(group - 1):
            @pl.when(i0 + g - j < 0)
            def _():
                carry_ref[g] = jnp.full(carry_ref.shape[1:], PARKED_LOG2, F32)
        block(j, False)
        return j + 1, any_row_alive()

    lax.while_loop(lambda state: (i0 + group - 1 - state[0] >= 0) & (state[1] > 0), body, (1, any_row_alive()))
    for g in range(group):
        o_ref[0, g * tq:(g + 1) * tq, :] = acc_ref[g].astype(BF16)


def _suffix_sum_matrix(tq):
    j = jnp.arange(2 * tq)[:, None] % tq
    s = jnp.arange(2 * tq)[None, :]
    return jnp.where((s >= tq) | (j > s), 1.0, 0.0).astype(BF16)


def _attn(q, k, v, *, tq=LANES, group=2):
    b, s, width = q.shape
    pairs = width // LANES
    assert tq == LANES and s % (group * tq) == 0 and width % LANES == 0
    q_spec = pl.BlockSpec((1, group * tq, width), lambda bi, i: (bi, i, 0))
    kv_spec = pl.BlockSpec((1, s, width), lambda bi, i: (bi, 0, 0))
    return pl.pallas_call(
        functools.partial(_attn_kernel, tq=tq, pairs=pairs, group=group),
        grid=(b, s // (group * tq)),
        in_specs=[q_spec, kv_spec, kv_spec, _resident((2 * tq, 2 * tq))],
        out_specs=q_spec,
        out_shape=jax.ShapeDtypeStruct((b, s, width), BF16),
        scratch_shapes=[pltpu.VMEM((group, pairs, 2 * tq, LANES), BF16),
                        pltpu.VMEM((group, pairs, 2 * tq, tq), F32),
                        pltpu.VMEM((group, tq, width), F32)],
        compiler_params=pltpu.CompilerParams(
            dimension_semantics=("parallel", "parallel"), vmem_limit_bytes=32 * MIB),
        name="attn",
    )(q, k, v, _suffix_sum_matrix(tq))


def _shift_rows(u, prev, shift):
    rolled = pltpu.roll(u, shift, axis=0)
    tail = pltpu.roll(prev, shift, axis=0)
    first = lax.broadcasted_iota(jnp.int32, tail.shape, 0) < shift
    head = jnp.where(first, tail, rolled[:SUBLANES])
    return jnp.concatenate([head, rolled[SUBLANES:]], axis=0)


def _mix_out_kernel(x_ref, y_ref, g_ref, wc_ref, wgc_ref, wga_ref, bgc_ref, bga_ref, cw_ref, wco_ref, wao_ref,
                    wo_ref, o_ref, tail_ref, *, cw_width):
    @pl.when(pl.program_id(1) == 0)
    def _():
        tail_ref[...] = jnp.zeros_like(tail_ref)

    x = x_ref[...]
    h = _rmsnorm(x, g_ref[...]).astype(BF16)
    cb = _dot(h, wc_ref[:, 0:cw_width])
    cc = _dot(h, wc_ref[:, cw_width:2 * cw_width])
    cx = _dot(h, wc_ref[:, 2 * cw_width:3 * cw_width])
    u = cc * cx
    prev = tail_ref[...]
    tail_ref[...] = u[u.shape[0] - SUBLANES:]
    cw = cw_ref[...]
    conv = cw[0:1] * _shift_rows(u, prev, 2) + cw[1:2] * _shift_rows(u, prev, 1) + cw[2:3] * u
    conv_branch = _dot((cb * conv).astype(BF16), wco_ref[...])
    attn_branch = _dot(y_ref[...], wao_ref[...])
    g_conv = jax.nn.sigmoid(_dot(h, wgc_ref[...]) + bgc_ref[...])
    g_attn = jax.nn.sigmoid(_dot(h, wga_ref[...]) + bga_ref[...])
    merged = g_conv * conv_branch + g_attn * attn_branch
    o_ref[...] = x + _dot(merged.astype(BF16), wo_ref[...])


def _mix_out(x, attn_y, g, w_in, b_gate, conv_w, w_conv_o, w_attn_o, w_out, *, seq, tm=512):
    n, d = x.shape
    cw_width = w_conv_o.shape[0]
    aw = attn_y.shape[1]
    gate_col = (3 * cw_width + 3 * aw) // d
    assert seq % tm == 0 and n % seq == 0 and conv_w.shape == (CONV_K, cw_width)
    assert w_in.shape == (d, (gate_col + 2) * d)
    tiles = seq // tm
    row = lambda width: pl.BlockSpec((tm, width), lambda bi, j: (bi * tiles + j, 0))
    return pl.pallas_call(
        functools.partial(_mix_out_kernel, cw_width=cw_width),
        grid=(n // seq, tiles),
        in_specs=[row(d), row(aw), _resident((1, d)), _resident((d, 3 * cw_width)),
                  _resident((d, d), gate_col), _resident((d, d), gate_col + 1),
                  _resident((1, d)), _resident((1, d), 1), _resident((CONV_K, cw_width)),
                  _resident((cw_width, d)), _resident((aw, d)), _resident((d, d))],
        out_specs=row(d),
        out_shape=jax.ShapeDtypeStruct((n, d), F32),
        scratch_shapes=[pltpu.VMEM((SUBLANES, cw_width), F32)],
        compiler_params=pltpu.CompilerParams(
            dimension_semantics=("arbitrary", "arbitrary"), vmem_limit_bytes=48 * MIB),
        name="mix_out",
    )(x, attn_y, g, w_in, w_in, w_in, b_gate, b_gate, conv_w, w_conv_o, w_attn_o, w_out)


def kernel(x, ffn1_norm, ffn1_w13, ffn1_w2, mix_norm, w_in, b_gate, conv_w, w_conv_o, w_attn_o, w_out,
           ffn2_norm, ffn2_w13, ffn2_w2, final_norm):
    b, s, d = x.shape
    depth = w_in.shape[0]
    cw_width = w_conv_o.shape[1]
    aw = w_attn_o.shape[1]
    assert cw_width == aw and (3 * cw_width + 3 * aw) % d == 0
    gf = final_norm.reshape(1, d)
    xf = x.reshape(b * s, d)
    for l in range(depth):
        w_in_l = w_in[l].astype(BF16)
        xf = _ffn(xf, ffn1_norm[l].reshape(1, d), ffn1_w13[l].astype(BF16), ffn1_w2[l].astype(BF16), gf,
                  final_norm=False)
        q, k, v = _qkv(xf, mix_norm[l].reshape(1, d), w_in_l, width=aw, col_block=1)
        attn_y = _attn(q.reshape(b, s, aw), k.reshape(b, s, aw), v.reshape(b, s, aw)).reshape(b * s, aw)
        xf = _mix_out(xf, attn_y, mix_norm[l].reshape(1, d), w_in_l,
                      b_gate[l].reshape(1, 2 * d), conv_w[l], w_conv_o[l].astype(BF16),
                      w_attn_o[l].astype(BF16), w_out[l].astype(BF16), seq=s)
        xf = _ffn(xf, ffn2_norm[l].reshape(1, d), ffn2_w13[l].astype(BF16), ffn2_w2[l].astype(BF16), gf,
                  final_norm=(l == depth - 1))
    return xf.reshape(b, s, d)
```

```python
import functools
import math

import jax
import jax.numpy as jnp
from jax import lax
from jax.experimental import pallas as pl
from jax.experimental.pallas import tpu as pltpu

F32 = jnp.float32
BF16 = jnp.bfloat16

RMS_EPS = 1e-6
LOG2E = 1.4426950408889634
DEAD_LOG2 = -160.0
PARKED_LOG2 = -1e30
HEAD_DIM = 64
CONV_K = 3
LANES = 128
SUBLANES = 8
W_CHUNK = 256
MIB = 1024 * 1024


def _resident(shape, col_block=0):
    index = (0,) * (len(shape) - 1) + (col_block,)
    return pl.BlockSpec(shape, lambda *_: index, pipeline_mode=pl.Buffered(1))


def _rmsnorm(x, g):
    ms = jnp.mean(x * x, axis=-1, keepdims=True)
    return x * lax.rsqrt(ms + RMS_EPS) * g


def _dot(a, b):
    return jnp.dot(a, b, preferred_element_type=F32)


def _load_as_bf16(chunks, stage_ref, sem):
    def copy(k):
        return pltpu.make_async_copy(chunks[k][0], stage_ref.at[k % 2], sem.at[k % 2])

    copy(0).start()
    for k in range(len(chunks)):
        if k + 1 < len(chunks):
            copy(k + 1).start()
        copy(k).wait()
        chunks[k][1][...] = stage_ref[k % 2].astype(BF16)


def _ffn_kernel(x_ref, g_ref, w13_hbm, w2_hbm, gf_ref, o_ref, w13_ref, w2_ref, act_ref, stage13_ref, stage2_ref,
                sem, *, layer, d_ff, ff_chunk, final_norm):
    @pl.when(pl.program_id(0) == 0)
    def _():
        _load_as_bf16([(w13_hbm.at[layer, :, pl.ds(c, W_CHUNK)], w13_ref.at[:, pl.ds(c, W_CHUNK)])
                       for c in range(0, 2 * d_ff, W_CHUNK)], stage13_ref, sem)
        _load_as_bf16([(w2_hbm.at[layer, pl.ds(r, W_CHUNK), :], w2_ref.at[pl.ds(r, W_CHUNK), :])
                       for r in range(0, d_ff, W_CHUNK)], stage2_ref, sem)

    x = x_ref[...]
    h = _rmsnorm(x, g_ref[...]).astype(BF16)
    for c in range(0, d_ff, ff_chunk):
        a = _dot(h, w13_ref[:, c:c + ff_chunk])
        b = _dot(h, w13_ref[:, d_ff + c:d_ff + c + ff_chunk])
        act_ref[:, c:c + ff_chunk] = (a * jax.nn.sigmoid(a) * b).astype(BF16)
    y = x + 0.5 * _dot(act_ref[...], w2_ref[...])
    if final_norm:
        y = _rmsnorm(y, gf_ref[...])
    o_ref[...] = y


def _ffn(x, g, w13, w2, gf, *, layer, final_norm, tm=512, ff_chunk=256):
    n, d = x.shape
    d_ff = w2.shape[1]
    assert n % tm == 0 and d_ff % ff_chunk == 0 and d_ff % W_CHUNK == 0 and w13.shape[1:] == (d, 2 * d_ff)
    row = pl.BlockSpec((tm, d), lambda i: (i, 0))
    hbm = pl.BlockSpec(memory_space=pl.ANY)
    return pl.pallas_call(
        functools.partial(_ffn_kernel, layer=layer, d_ff=d_ff, ff_chunk=ff_chunk, final_norm=final_norm),
        grid=(n // tm,),
        in_specs=[row, _resident((1, d)), hbm, hbm, _resident((1, d))],
        out_specs=row,
        out_shape=jax.ShapeDtypeStruct((n, d), F32),
        scratch_shapes=[pltpu.VMEM((d, 2 * d_ff), BF16), pltpu.VMEM((d_ff, d), BF16), pltpu.VMEM((tm, d_ff), BF16),
                        pltpu.VMEM((2, d, W_CHUNK), F32), pltpu.VMEM((2, W_CHUNK, d), F32),
                        pltpu.SemaphoreType.DMA((2,))],
        compiler_params=pltpu.CompilerParams(
            dimension_semantics=("arbitrary",), vmem_limit_bytes=52 * MIB),
        name="ffn",
    )(x, g, w13, w2, gf)


def _qkv_kernel(x_ref, g_ref, w_ref, q_ref, k_ref, v_ref, *, width, scale):
    h = _rmsnorm(x_ref[...], g_ref[...]).astype(BF16)
    q_ref[...] = (_dot(h, w_ref[:, 0:width]) * scale).astype(BF16)
    k_ref[...] = _dot(h, w_ref[:, width:2 * width]).astype(BF16)
    v_ref[...] = _dot(h, w_ref[:, 2 * width:3 * width]).astype(BF16)


def _qkv(x, g, w_in, *, width, col_block, tm=1024):
    n, d = x.shape
    assert n % tm == 0
    out = jax.ShapeDtypeStruct((n, width), BF16)
    out_spec = pl.BlockSpec((tm, width), lambda i: (i, 0))
    return pl.pallas_call(
        functools.partial(_qkv_kernel, width=width, scale=LOG2E / math.sqrt(HEAD_DIM)),
        grid=(n // tm,),
        in_specs=[pl.BlockSpec((tm, d), lambda i: (i, 0)), _resident((1, d)),
                  _resident((d, 3 * width), col_block)],
        out_specs=[out_spec, out_spec, out_spec],
        out_shape=[out, out, out],
        compiler_params=pltpu.CompilerParams(
            dimension_semantics=("parallel",), vmem_limit_bytes=32 * MIB),
        name="qkv",
    )(x, g, w_in)


def _attn_kernel(q_ref, k_ref, v_ref, w_ref, o_ref, qm_ref, carry_ref, acc_ref, *, tq, pairs, group):
    i0 = pl.program_id(1) * group
    first_head = lax.broadcasted_iota(jnp.int32, (tq, LANES), 1) < HEAD_DIM
    zeros = jnp.zeros((tq, LANES), BF16)
    lanes = [slice(p * LANES, (p + 1) * LANES) for p in range(pairs)]
    chains = [(g, p) for g in range(group) for p in range(pairs)]

    def split_heads(t):
        return jnp.concatenate([jnp.where(first_head, t, zeros), jnp.where(first_head, zeros, t)], axis=0)

    row = lax.broadcasted_iota(jnp.int32, (2 * tq, tq), 0) & (tq - 1)
    col = lax.broadcasted_iota(jnp.int32, (2 * tq, tq), 1)
    causal = col < row

    def block(j, diagonal):
        starts = [pl.multiple_of(jnp.maximum(i0 + g - j, 0) * tq, tq) for g in range(group)]
        zs = [lax.dot_general(qm_ref[g, p], k_ref[0, pl.ds(starts[g], tq), lanes[p]],
                              (((1,), (1,)), ((), ())), preferred_element_type=F32)
              for g, p in chains]
        log_betas, sums = [], []
        for z in zs:
            neg_part = jnp.minimum(z, 0.0)
            neg_relu = neg_part - z
            sp = jnp.log2(1.0 + jnp.exp2(neg_part + neg_relu))
            log_beta = neg_part - sp
            log_keep = neg_relu - sp
            if diagonal:
                log_keep = jnp.where(causal, log_keep, 0.0)
            hi = log_keep.astype(BF16)
            lo = (log_keep - hi.astype(F32)).astype(BF16)
            log_betas.append(log_beta)
            sums.append(_dot(jnp.concatenate([hi, lo], axis=1), w_ref[...]))
        for n, (g, p) in enumerate(chains):
            carry = carry_ref[g, p]
            a = jnp.exp2(log_betas[n] + (carry + sums[n][:, :tq]))
            if diagonal:
                a = jnp.where(causal, a, 0.0)
            carry_ref[g, p] = carry + sums[n][:, tq:]
            ab = a.astype(BF16)
            a_pair = jnp.concatenate([ab[:tq], ab[tq:]], axis=1)
            acc_ref[g, :, lanes[p]] += _dot(a_pair, split_heads(v_ref[0, pl.ds(starts[g], tq), lanes[p]]))

    for g, p in chains:
        qm_ref[g, p] = split_heads(q_ref[0, g * tq:(g + 1) * tq, lanes[p]])
    carry_ref[...] = jnp.zeros_like(carry_ref)
    acc_ref[...] = jnp.zeros_like(acc_ref)
    block(0, True)

    def any_row_alive():
        live = carry_ref[0, 0]
        for g, p in chains[1:]:
            live = jnp.maximum(live, carry_ref[g, p])
        return (jnp.max(live) > DEAD_LOG2).astype(jnp.int32)

    def body(state):
        j, _ = state
        for g in range(group - 1):
            @pl.when(i0 + g - j < 0)
            def _():
                carry_ref[g] = jnp.full(carry_ref.shape[1:], PARKED_LOG2, F32)
        block(j, False)
        return j + 1, any_row_alive()

    lax.while_loop(lambda state: (i0 + group - 1 - state[0] >= 0) & (state[1] > 0), body, (1, any_row_alive()))
    for g in range(group):
        o_ref[0, g * tq:(g + 1) * tq, :] = acc_ref[g].astype(BF16)


def _suffix_sum_matrix(tq):
    j = jnp.arange(2 * tq)[:, None] % tq
    s = jnp.arange(2 * tq)[None, :]
    return jnp.where((s >= tq) | (j > s), 1.0, 0.0).astype(BF16)


def _attn(q, k, v, *, tq=LANES, group=2):
    b, s, width = q.shape
    pairs = width // LANES
    assert tq == LANES and s % (group * tq) == 0 and width % LANES == 0
    q_spec = pl.BlockSpec((1, group * tq, width), lambda bi, i: (bi, i, 0))
    kv_spec = pl.BlockSpec((1, s, width), lambda bi, i: (bi, 0, 0))
    return pl.pallas_call(
        functools.partial(_attn_kernel, tq=tq, pairs=pairs, group=group),
        grid=(b, s // (group * tq)),
        in_specs=[q_spec, kv_spec, kv_spec, _resident((2 * tq, 2 * tq))],
        out_specs=q_spec,
        out_shape=jax.ShapeDtypeStruct((b, s, width), BF16),
        scratch_shapes=[pltpu.VMEM((group, pairs, 2 * tq, LANES), BF16),
                        pltpu.VMEM((group, pairs, 2 * tq, tq), F32),
                        pltpu.VMEM((group, tq, width), F32)],
        compiler_params=pltpu.CompilerParams(
            dimension_semantics=("parallel", "parallel"), vmem_limit_bytes=32 * MIB),
        name="attn",
    )(q, k, v, _suffix_sum_matrix(tq))


def _shift_rows(u, prev, shift):
    rolled = pltpu.roll(u, shift, axis=0)
    tail = pltpu.roll(prev, shift, axis=0)
    first = lax.broadcasted_iota(jnp.int32, tail.shape, 0) < shift
    head = jnp.where(first, tail, rolled[:SUBLANES])
    return jnp.concatenate([head, rolled[SUBLANES:]], axis=0)


def _mix_out_kernel(x_ref, y_ref, g_ref, wc_ref, wgc_ref, wga_ref, bgc_ref, bga_ref, cw_ref, wco_ref, wao_ref,
                    wo_ref, o_ref, tail_ref, conv_y_ref, merged_ref, *, cw_width, chunk):
    @pl.when(pl.program_id(1) == 0)
    def _():
        tail_ref[...] = jnp.zeros_like(tail_ref)

    x = x_ref[...]
    tm, d = x.shape
    h = _rmsnorm(x, g_ref[...]).astype(BF16)
    for c in range(0, cw_width, chunk):
        cols = slice(c, c + chunk)
        cb = _dot(h, wc_ref[:, c:c + chunk])
        cc = _dot(h, wc_ref[:, cw_width + c:cw_width + c + chunk])
        cx = _dot(h, wc_ref[:, 2 * cw_width + c:2 * cw_width + c + chunk])
        u = cc * cx
        prev = tail_ref[:, cols]
        tail_ref[:, cols] = u[tm - SUBLANES:]
        cw = cw_ref[:, cols]
        conv = cw[0:1] * _shift_rows(u, prev, 2) + cw[1:2] * _shift_rows(u, prev, 1) + cw[2:3] * u
        conv_y_ref[:, cols] = (cb * conv).astype(BF16)
    for c in range(0, d, chunk):
        cols = slice(c, c + chunk)
        g_conv = jax.nn.sigmoid(_dot(h, wgc_ref[:, cols]) + bgc_ref[:, cols])
        g_attn = jax.nn.sigmoid(_dot(h, wga_ref[:, cols]) + bga_ref[:, cols])
        merged = g_conv * _dot(conv_y_ref[...], wco_ref[:, cols]) + g_attn * _dot(y_ref[...], wao_ref[:, cols])
        merged_ref[:, cols] = merged.astype(BF16)
    o_ref[...] = x + _dot(merged_ref[...], wo_ref[...])


def _mix_out(x, attn_y, g, w_in, b_gate, conv_w, w_conv_o, w_attn_o, w_out, *, seq, tm=512, chunk=256):
    n, d = x.shape
    cw_width = w_conv_o.shape[0]
    aw = attn_y.shape[1]
    gate_col = (3 * cw_width + 3 * aw) // d
    assert seq % tm == 0 and n % seq == 0 and conv_w.shape == (CONV_K, cw_width)
    assert w_in.shape == (d, (gate_col + 2) * d) and cw_width % chunk == 0 and d % chunk == 0
    tiles = seq // tm
    row = lambda width: pl.BlockSpec((tm, width), lambda bi, j: (bi * tiles + j, 0))
    return pl.pallas_call(
        functools.partial(_mix_out_kernel, cw_width=cw_width, chunk=chunk),
        grid=(n // seq, tiles),
        in_specs=[row(d), row(aw), _resident((1, d)), _resident((d, 3 * cw_width)),
                  _resident((d, d), gate_col), _resident((d, d), gate_col + 1),
                  _resident((1, d)), _resident((1, d), 1), _resident((CONV_K, cw_width)),
                  _resident((cw_width, d)), _resident((aw, d)), _resident((d, d))],
        out_specs=row(d),
        out_shape=jax.ShapeDtypeStruct((n, d), F32),
        scratch_shapes=[pltpu.VMEM((SUBLANES, cw_width), F32), pltpu.VMEM((tm, cw_width), BF16),
                        pltpu.VMEM((tm, d), BF16)],
        compiler_params=pltpu.CompilerParams(
            dimension_semantics=("arbitrary", "arbitrary"), vmem_limit_bytes=48 * MIB),
        name="mix_out",
    )(x, attn_y, g, w_in, w_in, w_in, b_gate, b_gate, conv_w, w_conv_o, w_attn_o, w_out)


def kernel(x, ffn1_norm, ffn1_w13, ffn1_w2, mix_norm, w_in, b_gate, conv_w, w_conv_o, w_attn_o, w_out,
           ffn2_norm, ffn2_w13, ffn2_w2, final_norm):
    b, s, d = x.shape
    depth = w_in.shape[0]
    cw_width = w_conv_o.shape[1]
    aw = w_attn_o.shape[1]
    assert cw_width == aw and (3 * cw_width + 3 * aw) % d == 0
    gf = final_norm.reshape(1, d)
    xf = x.reshape(b * s, d)
    for l in range(depth):
        w_in_l = w_in[l].astype(BF16)
        xf = _ffn(xf, ffn1_norm[l].reshape(1, d), ffn1_w13, ffn1_w2, gf, layer=l, final_norm=False)
        q, k, v = _qkv(xf, mix_norm[l].reshape(1, d), w_in_l, width=aw, col_block=1)
        attn_y = _attn(q.reshape(b, s, aw), k.reshape(b, s, aw), v.reshape(b, s, aw)).reshape(b * s, aw)
        xf = _mix_out(xf, attn_y, mix_norm[l].reshape(1, d), w_in_l,
                      b_gate[l].reshape(1, 2 * d), conv_w[l], w_conv_o[l].astype(BF16),
                      w_attn_o[l].astype(BF16), w_out[l].astype(BF16), seq=s)
        xf = _ffn(xf, ffn2_norm[l].reshape(1, d), ffn2_w13, ffn2_w2, gf, layer=l, final_norm=(l == depth - 1))
    return xf.reshape(b, s, d)
```

```python
import functools
import math

import jax
import jax.numpy as jnp
from jax import lax
from jax.experimental import pallas as pl
from jax.experimental.pallas import tpu as pltpu

F32 = jnp.float32
BF16 = jnp.bfloat16

RMS_EPS = 1e-6
LOG2E = 1.4426950408889634
DEAD_LOG2 = -160.0
PARKED_LOG2 = -1e30
HEAD_DIM = 64
CONV_K = 3
LANES = 128
SUBLANES = 8
W13_ROWS = 64
W_SLOTS = 3
MIB = 1024 * 1024


def _resident(shape, col_block=0):
    index = (0,) * (len(shape) - 1) + (col_block,)
    return pl.BlockSpec(shape, lambda *_: index, pipeline_mode=pl.Buffered(1))


def _rmsnorm(x, g):
    ms = jnp.mean(x * x, axis=-1, keepdims=True)
    return x * lax.rsqrt(ms + RMS_EPS) * g


def _dot(a, b):
    return jnp.dot(a, b, preferred_element_type=F32)


def _load_as_bf16(chunks, stage_ref, sem):
    slots = stage_ref.shape[0]

    def copy(k):
        return pltpu.make_async_copy(chunks[k][0], stage_ref.at[k % slots], sem.at[k % slots])

    for k in range(min(slots - 1, len(chunks))):
        copy(k).start()
    for k in range(len(chunks)):
        if k + slots - 1 < len(chunks):
            copy(k + slots - 1).start()
        copy(k).wait()
        chunks[k][1][...] = stage_ref[k % slots].astype(BF16)


def _ffn_kernel(x_ref, g_ref, w13_hbm, w2_hbm, gf_ref, o_ref, w13_ref, w2_ref, act_ref, stage13_ref, stage2_ref,
                sem, *, layer, d_ff, ff_chunk, final_norm):
    @pl.when(pl.program_id(0) == 0)
    def _():
        rows13, rows2 = stage13_ref.shape[1], stage2_ref.shape[1]
        _load_as_bf16([(w13_hbm.at[layer, pl.ds(r, rows13), :], w13_ref.at[pl.ds(r, rows13), :])
                       for r in range(0, w13_ref.shape[0], rows13)], stage13_ref, sem)
        _load_as_bf16([(w2_hbm.at[layer, pl.ds(r, rows2), :], w2_ref.at[pl.ds(r, rows2), :])
                       for r in range(0, d_ff, rows2)], stage2_ref, sem)

    x = x_ref[...]
    h = _rmsnorm(x, g_ref[...]).astype(BF16)
    for c in range(0, d_ff, ff_chunk):
        a = _dot(h, w13_ref[:, c:c + ff_chunk])
        b = _dot(h, w13_ref[:, d_ff + c:d_ff + c + ff_chunk])
        act_ref[:, c:c + ff_chunk] = (a * jax.nn.sigmoid(a) * b).astype(BF16)
    y = x + 0.5 * _dot(act_ref[...], w2_ref[...])
    if final_norm:
        y = _rmsnorm(y, gf_ref[...])
    o_ref[...] = y


def _ffn(x, g, w13, w2, gf, *, layer, final_norm, tm=512, ff_chunk=256):
    n, d = x.shape
    d_ff = w2.shape[1]
    rows13, rows2 = W13_ROWS, W13_ROWS * 2 * d_ff // d
    assert n % tm == 0 and d_ff % ff_chunk == 0 and w13.shape[1:] == (d, 2 * d_ff)
    assert d % rows13 == 0 and d_ff % rows2 == 0 and rows13 % 16 == 0 and rows2 % 16 == 0
    row = pl.BlockSpec((tm, d), lambda i: (i, 0))
    hbm = pl.BlockSpec(memory_space=pl.ANY)
    return pl.pallas_call(
        functools.partial(_ffn_kernel, layer=layer, d_ff=d_ff, ff_chunk=ff_chunk, final_norm=final_norm),
        grid=(n // tm,),
        in_specs=[row, _resident((1, d)), hbm, hbm, _resident((1, d))],
        out_specs=row,
        out_shape=jax.ShapeDtypeStruct((n, d), F32),
        scratch_shapes=[pltpu.VMEM((d, 2 * d_ff), BF16), pltpu.VMEM((d_ff, d), BF16), pltpu.VMEM((tm, d_ff), BF16),
                        pltpu.VMEM((W_SLOTS, rows13, 2 * d_ff), F32), pltpu.VMEM((W_SLOTS, rows2, d), F32),
                        pltpu.SemaphoreType.DMA((W_SLOTS,))],
        compiler_params=pltpu.CompilerParams(
            dimension_semantics=("arbitrary",), vmem_limit_bytes=52 * MIB),
        name="ffn",
    )(x, g, w13, w2, gf)


def _qkv_kernel(x_ref, g_ref, w_ref, q_ref, k_ref, v_ref, *, width, scale):
    h = _rmsnorm(x_ref[...], g_ref[...]).astype(BF16)
    q_ref[...] = (_dot(h, w_ref[:, 0:width]) * scale).astype(BF16)
    k_ref[...] = _dot(h, w_ref[:, width:2 * width]).astype(BF16)
    v_ref[...] = _dot(h, w_ref[:, 2 * width:3 * width]).astype(BF16)


def _qkv(x, g, w_in, *, width, col_block, tm=1024):
    n, d = x.shape
    assert n % tm == 0
    out = jax.ShapeDtypeStruct((n, width), BF16)
    out_spec = pl.BlockSpec((tm, width), lambda i: (i, 0))
    return pl.pallas_call(
        functools.partial(_qkv_kernel, width=width, scale=LOG2E / math.sqrt(HEAD_DIM)),
        grid=(n // tm,),
        in_specs=[pl.BlockSpec((tm, d), lambda i: (i, 0)), _resident((1, d)),
                  _resident((d, 3 * width), col_block)],
        out_specs=[out_spec, out_spec, out_spec],
        out_shape=[out, out, out],
        compiler_params=pltpu.CompilerParams(
            dimension_semantics=("parallel",), vmem_limit_bytes=32 * MIB),
        name="qkv",
    )(x, g, w_in)


def _attn_kernel(q_ref, k_ref, v_ref, w_ref, o_ref, qm_ref, carry_ref, acc_ref, *, tq, pairs, group):
    i0 = pl.program_id(1) * group
    first_head = lax.broadcasted_iota(jnp.int32, (tq, LANES), 1) < HEAD_DIM
    zeros = jnp.zeros((tq, LANES), BF16)
    lanes = [slice(p * LANES, (p + 1) * LANES) for p in range(pairs)]
    chains = [(g, p) for g in range(group) for p in range(pairs)]

    def split_heads(t):
        return jnp.concatenate([jnp.where(first_head, t, zeros), jnp.where(first_head, zeros, t)], axis=0)

    row = lax.broadcasted_iota(jnp.int32, (2 * tq, tq), 0) & (tq - 1)
    col = lax.broadcasted_iota(jnp.int32, (2 * tq, tq), 1)
    causal = col < row

    def block(j, diagonal):
        starts = [pl.multiple_of(jnp.maximum(i0 + g - j, 0) * tq, tq) for g in range(group)]
        zs = [lax.dot_general(qm_ref[g, p], k_ref[0, pl.ds(starts[g], tq), lanes[p]],
                              (((1,), (1,)), ((), ())), preferred_element_type=F32)
              for g, p in chains]
        log_betas, sums = [], []
        for z in zs:
            neg_part = jnp.minimum(z, 0.0)
            neg_relu = neg_part - z
            sp = jnp.log2(1.0 + jnp.exp2(neg_part + neg_relu))
            log_beta = neg_part - sp
            log_keep = neg_relu - sp
            if diagonal:
                log_keep = jnp.where(causal, log_keep, 0.0)
            hi = log_keep.astype(BF16)
            lo = (log_keep - hi.astype(F32)).astype(BF16)
            log_betas.append(log_beta)
            sums.append(_dot(jnp.concatenate([hi, lo], axis=1), w_ref[...]))
        for n, (g, p) in enumerate(chains):
            carry = carry_ref[g, p]
            a = jnp.exp2(log_betas[n] + (carry + sums[n][:, :tq]))
            if diagonal:
                a = jnp.where(causal, a, 0.0)
            carry_ref[g, p] = carry + sums[n][:, tq:]
            ab = a.astype(BF16)
            a_pair = jnp.concatenate([ab[:tq], ab[tq:]], axis=1)
            acc_ref[g, :, lanes[p]] += _dot(a_pair, split_heads(v_ref[0, pl.ds(starts[g], tq), lanes[p]]))

    for g, p in chains:
        qm_ref[g, p] = split_heads(q_ref[0, g * tq:(g + 1) * tq, lanes[p]])
    carry_ref[...] = jnp.zeros_like(carry_ref)
    acc_ref[...] = jnp.zeros_like(acc_ref)
    block(0, True)

    def any_row_alive():
        live = carry_ref[0, 0]
        for g, p in chains[1:]:
            live = jnp.maximum(live, carry_ref[g, p])
        return (jnp.max(live) > DEAD_LOG2).astype(jnp.int32)

    def body(state):
        j, _ = state
        for g in range(group - 1):
            @pl.when(i0 + g - j < 0)
            def _():
                carry_ref[g] = jnp.full(carry_ref.shape[1:], PARKED_LOG2, F32)
        block(j, False)
        return j + 1, any_row_alive()

    lax.while_loop(lambda state: (i0 + group - 1 - state[0] >= 0) & (state[1] > 0), body, (1, any_row_alive()))
    for g in range(group):
        o_ref[0, g * tq:(g + 1) * tq, :] = acc_ref[g].astype(BF16)


def _suffix_sum_matrix(tq):
    j = jnp.arange(2 * tq)[:, None] % tq
    s = jnp.arange(2 * tq)[None, :]
    return jnp.where((s >= tq) | (j > s), 1.0, 0.0).astype(BF16)


def _attn(q, k, v, *, tq=LANES, group=2):
    b, s, width = q.shape
    pairs = width // LANES
    assert tq == LANES and s % (group * tq) == 0 and width % LANES == 0
    q_spec = pl.BlockSpec((1, group * tq, width), lambda bi, i: (bi, i, 0))
    kv_spec = pl.BlockSpec((1, s, width), lambda bi, i: (bi, 0, 0))
    return pl.pallas_call(
        functools.partial(_attn_kernel, tq=tq, pairs=pairs, group=group),
        grid=(b, s // (group * tq)),
        in_specs=[q_spec, kv_spec, kv_spec, _resident((2 * tq, 2 * tq))],
        out_specs=q_spec,
        out_shape=jax.ShapeDtypeStruct((b, s, width), BF16),
        scratch_shapes=[pltpu.VMEM((group, pairs, 2 * tq, LANES), BF16),
                        pltpu.VMEM((group, pairs, 2 * tq, tq), F32),
                        pltpu.VMEM((group, tq, width), F32)],
        compiler_params=pltpu.CompilerParams(
            dimension_semantics=("parallel", "parallel"), vmem_limit_bytes=32 * MIB),
        name="attn",
    )(q, k, v, _suffix_sum_matrix(tq))


def _shift_rows(u, prev, shift):
    rolled = pltpu.roll(u, shift, axis=0)
    tail = pltpu.roll(prev, shift, axis=0)
    first = lax.broadcasted_iota(jnp.int32, tail.shape, 0) < shift
    head = jnp.where(first, tail, rolled[:SUBLANES])
    return jnp.concatenate([head, rolled[SUBLANES:]], axis=0)


def _mix_out_kernel(x_ref, y_ref, g_ref, wc_ref, wgc_ref, wga_ref, bgc_ref, bga_ref, cw_ref, wco_ref, wao_ref,
                    wo_ref, o_ref, tail_ref, conv_y_ref, merged_ref, *, cw_width, chunk):
    @pl.when(pl.program_id(1) == 0)
    def _():
        tail_ref[...] = jnp.zeros_like(tail_ref)

    x = x_ref[...]
    tm, d = x.shape
    h = _rmsnorm(x, g_ref[...]).astype(BF16)
    for c in range(0, cw_width, chunk):
        cols = slice(c, c + chunk)
        cb = _dot(h, wc_ref[:, c:c + chunk])
        cc = _dot(h, wc_ref[:, cw_width + c:cw_width + c + chunk])
        cx = _dot(h, wc_ref[:, 2 * cw_width + c:2 * cw_width + c + chunk])
        u = cc * cx
        prev = tail_ref[:, cols]
        tail_ref[:, cols] = u[tm - SUBLANES:]
        cw = cw_ref[:, cols]
        conv = cw[0:1] * _shift_rows(u, prev, 2) + cw[1:2] * _shift_rows(u, prev, 1) + cw[2:3] * u
        conv_y_ref[:, cols] = (cb * conv).astype(BF16)
    for c in range(0, d, chunk):
        cols = slice(c, c + chunk)
        g_conv = jax.nn.sigmoid(_dot(h, wgc_ref[:, cols]) + bgc_ref[:, cols])
        g_attn = jax.nn.sigmoid(_dot(h, wga_ref[:, cols]) + bga_ref[:, cols])
        merged = g_conv * _dot(conv_y_ref[...], wco_ref[:, cols]) + g_attn * _dot(y_ref[...], wao_ref[:, cols])
        merged_ref[:, cols] = merged.astype(BF16)
    o_ref[...] = x + _dot(merged_ref[...], wo_ref[...])


def _mix_out(x, attn_y, g, w_in, b_gate, conv_w, w_conv_o, w_attn_o, w_out, *, seq, tm=512, chunk=256):
    n, d = x.shape
    cw_width = w_conv_o.shape[0]
    aw = attn_y.shape[1]
    gate_col = (3 * cw_width + 3 * aw) // d
    assert seq % tm == 0 and n % seq == 0 and conv_w.shape == (CONV_K, cw_width)
    assert w_in.shape == (d, (gate_col + 2) * d) and cw_width % chunk == 0 and d % chunk == 0
    tiles = seq // tm
    row = lambda width: pl.BlockSpec((tm, width), lambda bi, j: (bi * tiles + j, 0))
    return pl.pallas_call(
        functools.partial(_mix_out_kernel, cw_width=cw_width, chunk=chunk),
        grid=(n // seq, tiles),
        in_specs=[row(d), row(aw), _resident((1, d)), _resident((d, 3 * cw_width)),
                  _resident((d, d), gate_col), _resident((d, d), gate_col + 1),
                  _resident((1, d)), _resident((1, d), 1), _resident((CONV_K, cw_width)),
                  _resident((cw_width, d)), _resident((aw, d)), _resident((d, d))],
        out_specs=row(d),
        out_shape=jax.ShapeDtypeStruct((n, d), F32),
        scratch_shapes=[pltpu.VMEM((SUBLANES, cw_width), F32), pltpu.VMEM((tm, cw_width), BF16),
                        pltpu.VMEM((tm, d), BF16)],
        compiler_params=pltpu.CompilerParams(
            dimension_semantics=("arbitrary", "arbitrary"), vmem_limit_bytes=48 * MIB),
        name="mix_out",
    )(x, attn_y, g, w_in, w_in, w_in, b_gate, b_gate, conv_w, w_conv_o, w_attn_o, w_out)


def kernel(x, ffn1_norm, ffn1_w13, ffn1_w2, mix_norm, w_in, b_gate, conv_w, w_conv_o, w_attn_o, w_out,
           ffn2_norm, ffn2_w13, ffn2_w2, final_norm):
    b, s, d = x.shape
    depth = w_in.shape[0]
    cw_width = w_conv_o.shape[1]
    aw = w_attn_o.shape[1]
    assert cw_width == aw and (3 * cw_width + 3 * aw) % d == 0
    gf = final_norm.reshape(1, d)
    xf = x.reshape(b * s, d)
    for l in range(depth):
        w_in_l = w_in[l].astype(BF16)
        xf = _ffn(xf, ffn1_norm[l].reshape(1, d), ffn1_w13, ffn1_w2, gf, layer=l, final_norm=False)
        q, k, v = _qkv(xf, mix_norm[l].reshape(1, d), w_in_l, width=aw, col_block=1)
        attn_y = _attn(q.reshape(b, s, aw), k.reshape(b, s, aw), v.reshape(b, s, aw)).reshape(b * s, aw)
        xf = _mix_out(xf, attn_y, mix_norm[l].reshape(1, d), w_in_l,
                      b_gate[l].reshape(1, 2 * d), conv_w[l], w_conv_o[l].astype(BF16),
                      w_attn_o[l].astype(BF16), w_out[l].astype(BF16), seq=s)
        xf = _ffn(xf, ffn2_norm[l].reshape(1, d), ffn2_w13, ffn2_w2, gf, layer=l, final_norm=(l == depth - 1))
    return xf.reshape(b, s, d)
```

```python
import functools
import math

import jax
import jax.numpy as jnp
from jax import lax
from jax.experimental import pallas as pl
from jax.experimental.pallas import tpu as pltpu

F32 = jnp.float32
BF16 = jnp.bfloat16

RMS_EPS = 1e-6
LOG2E = 1.4426950408889634
DEAD_LOG2 = -160.0
PARKED_LOG2 = -1e30
HEAD_DIM = 64
CONV_K = 3
LANES = 128
SUBLANES = 8
W13_ROWS = 64
W_SLOTS = 3
MIB = 1024 * 1024


def _resident(shape, col_block=0):
    index = (0,) * (len(shape) - 1) + (col_block,)
    return pl.BlockSpec(shape, lambda *_: index, pipeline_mode=pl.Buffered(1))


def _rmsnorm(x, g):
    ms = jnp.mean(x * x, axis=-1, keepdims=True)
    return x * lax.rsqrt(ms + RMS_EPS) * g


def _dot(a, b):
    return jnp.dot(a, b, preferred_element_type=F32)


def _load_as_bf16(chunks, stage_ref, sem):
    slots = stage_ref.shape[0]

    def copy(k):
        return pltpu.make_async_copy(chunks[k][0], stage_ref.at[k % slots], sem.at[k % slots])

    for k in range(min(slots - 1, len(chunks))):
        copy(k).start()
    for k in range(len(chunks)):
        if k + slots - 1 < len(chunks):
            copy(k + slots - 1).start()
        copy(k).wait()
        chunks[k][1][...] = stage_ref[k % slots].astype(BF16)


def _ffn_kernel(x_ref, g_ref, w13_hbm, w2_hbm, gf_ref, o_ref, w13_ref, w2_ref, act_ref, stage13_ref, stage2_ref,
                sem, *, layer, d_ff, ff_chunk, final_norm):
    @pl.when(pl.program_id(0) == 0)
    def _():
        rows13, rows2 = stage13_ref.shape[1], stage2_ref.shape[1]
        _load_as_bf16([(w13_hbm.at[layer, pl.ds(r, rows13), :], w13_ref.at[pl.ds(r, rows13), :])
                       for r in range(0, w13_ref.shape[0], rows13)], stage13_ref, sem)
        _load_as_bf16([(w2_hbm.at[layer, pl.ds(r, rows2), :], w2_ref.at[pl.ds(r, rows2), :])
                       for r in range(0, d_ff, rows2)], stage2_ref, sem)

    x = x_ref[...]
    h = _rmsnorm(x, g_ref[...]).astype(BF16)
    for c in range(0, d_ff, ff_chunk):
        a = _dot(h, w13_ref[:, c:c + ff_chunk])
        b = _dot(h, w13_ref[:, d_ff + c:d_ff + c + ff_chunk])
        act_ref[:, c:c + ff_chunk] = (a * jax.nn.sigmoid(a) * b).astype(BF16)
    y = x + 0.5 * _dot(act_ref[...], w2_ref[...])
    if final_norm:
        y = _rmsnorm(y, gf_ref[...])
    o_ref[...] = y


def _ffn(x, g, w13, w2, gf, *, layer, final_norm, tm=1024, ff_chunk=256):
    n, d = x.shape
    d_ff = w2.shape[1]
    rows13, rows2 = W13_ROWS, W13_ROWS * 2 * d_ff // d
    assert n % tm == 0 and d_ff % ff_chunk == 0 and w13.shape[1:] == (d, 2 * d_ff)
    assert d % rows13 == 0 and d_ff % rows2 == 0 and rows13 % 16 == 0 and rows2 % 16 == 0
    row = pl.BlockSpec((tm, d), lambda i: (i, 0))
    hbm = pl.BlockSpec(memory_space=pl.ANY)
    return pl.pallas_call(
        functools.partial(_ffn_kernel, layer=layer, d_ff=d_ff, ff_chunk=ff_chunk, final_norm=final_norm),
        grid=(n // tm,),
        in_specs=[row, _resident((1, d)), hbm, hbm, _resident((1, d))],
        out_specs=row,
        out_shape=jax.ShapeDtypeStruct((n, d), F32),
        scratch_shapes=[pltpu.VMEM((d, 2 * d_ff), BF16), pltpu.VMEM((d_ff, d), BF16), pltpu.VMEM((tm, d_ff), BF16),
                        pltpu.VMEM((W_SLOTS, rows13, 2 * d_ff), F32), pltpu.VMEM((W_SLOTS, rows2, d), F32),
                        pltpu.SemaphoreType.DMA((W_SLOTS,))],
        compiler_params=pltpu.CompilerParams(
            dimension_semantics=("arbitrary",), vmem_limit_bytes=56 * MIB),
        name="ffn",
    )(x, g, w13, w2, gf)


def _qkv_kernel(x_ref, g_ref, w_ref, q_ref, k_ref, v_ref, *, width, scale):
    h = _rmsnorm(x_ref[...], g_ref[...]).astype(BF16)
    q_ref[...] = (_dot(h, w_ref[:, 0:width]) * scale).astype(BF16)
    k_ref[...] = _dot(h, w_ref[:, width:2 * width]).astype(BF16)
    v_ref[...] = _dot(h, w_ref[:, 2 * width:3 * width]).astype(BF16)


def _qkv(x, g, w_in, *, width, col_block, tm=1024):
    n, d = x.shape
    assert n % tm == 0
    out = jax.ShapeDtypeStruct((n, width), BF16)
    out_spec = pl.BlockSpec((tm, width), lambda i: (i, 0))
    return pl.pallas_call(
        functools.partial(_qkv_kernel, width=width, scale=LOG2E / math.sqrt(HEAD_DIM)),
        grid=(n // tm,),
        in_specs=[pl.BlockSpec((tm, d), lambda i: (i, 0)), _resident((1, d)),
                  _resident((d, 3 * width), col_block)],
        out_specs=[out_spec, out_spec, out_spec],
        out_shape=[out, out, out],
        compiler_params=pltpu.CompilerParams(
            dimension_semantics=("parallel",), vmem_limit_bytes=32 * MIB),
        name="qkv",
    )(x, g, w_in)


def _attn_kernel(q_ref, k_ref, v_ref, w_ref, o_ref, qm_ref, carry_ref, acc_ref, *, tq, pairs, group):
    i0 = pl.program_id(1) * group
    first_head = lax.broadcasted_iota(jnp.int32, (tq, LANES), 1) < HEAD_DIM
    zeros = jnp.zeros((tq, LANES), BF16)
    lanes = [slice(p * LANES, (p + 1) * LANES) for p in range(pairs)]
    chains = [(g, p) for g in range(group) for p in range(pairs)]

    def split_heads(t):
        return jnp.concatenate([jnp.where(first_head, t, zeros), jnp.where(first_head, zeros, t)], axis=0)

    row = lax.broadcasted_iota(jnp.int32, (2 * tq, tq), 0) & (tq - 1)
    col = lax.broadcasted_iota(jnp.int32, (2 * tq, tq), 1)
    causal = col < row

    def block(j, diagonal):
        starts = [pl.multiple_of(jnp.maximum(i0 + g - j, 0) * tq, tq) for g in range(group)]
        zs = [lax.dot_general(qm_ref[g, p], k_ref[0, pl.ds(starts[g], tq), lanes[p]],
                              (((1,), (1,)), ((), ())), preferred_element_type=F32)
              for g, p in chains]
        log_betas, sums = [], []
        for z in zs:
            neg_part = jnp.minimum(z, 0.0)
            neg_relu = neg_part - z
            sp = jnp.log2(1.0 + jnp.exp2(neg_part + neg_relu))
            log_beta = neg_part - sp
            log_keep = neg_relu - sp
            if diagonal:
                log_keep = jnp.where(causal, log_keep, 0.0)
            hi = log_keep.astype(BF16)
            lo = (log_keep - hi.astype(F32)).astype(BF16)
            log_betas.append(log_beta)
            sums.append(_dot(jnp.concatenate([hi, lo], axis=1), w_ref[...]))
        for n, (g, p) in enumerate(chains):
            carry = carry_ref[g, p]
            a = jnp.exp2(log_betas[n] + (carry + sums[n][:, :tq]))
            if diagonal:
                a = jnp.where(causal, a, 0.0)
            carry_ref[g, p] = carry + sums[n][:, tq:]
            ab = a.astype(BF16)
            a_pair = jnp.concatenate([ab[:tq], ab[tq:]], axis=1)
            acc_ref[g, :, lanes[p]] += _dot(a_pair, split_heads(v_ref[0, pl.ds(starts[g], tq), lanes[p]]))

    for g, p in chains:
        qm_ref[g, p] = split_heads(q_ref[0, g * tq:(g + 1) * tq, lanes[p]])
    carry_ref[...] = jnp.zeros_like(carry_ref)
    acc_ref[...] = jnp.zeros_like(acc_ref)
    block(0, True)

    def any_row_alive():
        live = carry_ref[0, 0]
        for g, p in chains[1:]:
            live = jnp.maximum(live, carry_ref[g, p])
        return (jnp.max(live) > DEAD_LOG2).astype(jnp.int32)

    def body(state):
        j, _ = state
        for g in range(group - 1):
            @pl.when(i0 + g - j < 0)
            def _():
                carry_ref[g] = jnp.full(carry_ref.shape[1:], PARKED_LOG2, F32)
        block(j, False)
        return j + 1, any_row_alive()

    lax.while_loop(lambda state: (i0 + group - 1 - state[0] >= 0) & (state[1] > 0), body, (1, jnp.int32(1)))
    for g in range(group):
        o_ref[0, g * tq:(g + 1) * tq, :] = acc_ref[g].astype(BF16)


def _suffix_sum_matrix(tq):
    j = jnp.arange(2 * tq)[:, None] % tq
    s = jnp.arange(2 * tq)[None, :]
    return jnp.where((s >= tq) | (j > s), 1.0, 0.0).astype(BF16)


def _attn(q, k, v, *, tq=LANES, group=2):
    b, s, width = q.shape
    pairs = width // LANES
    assert tq == LANES and s % (group * tq) == 0 and width % LANES == 0
    q_spec = pl.BlockSpec((1, group * tq, width), lambda bi, i: (bi, i, 0))
    kv_spec = pl.BlockSpec((1, s, width), lambda bi, i: (bi, 0, 0))
    return pl.pallas_call(
        functools.partial(_attn_kernel, tq=tq, pairs=pairs, group=group),
        grid=(b, s // (group * tq)),
        in_specs=[q_spec, kv_spec, kv_spec, _resident((2 * tq, 2 * tq))],
        out_specs=q_spec,
        out_shape=jax.ShapeDtypeStruct((b, s, width), BF16),
        scratch_shapes=[pltpu.VMEM((group, pairs, 2 * tq, LANES), BF16),
                        pltpu.VMEM((group, pairs, 2 * tq, tq), F32),
                        pltpu.VMEM((group, tq, width), F32)],
        compiler_params=pltpu.CompilerParams(
            dimension_semantics=("parallel", "parallel"), vmem_limit_bytes=32 * MIB),
        name="attn",
    )(q, k, v, _suffix_sum_matrix(tq))


def _shift_rows(u, prev, shift):
    rolled = pltpu.roll(u, shift, axis=0)
    tail = pltpu.roll(prev, shift, axis=0)
    first = lax.broadcasted_iota(jnp.int32, tail.shape, 0) < shift
    head = jnp.where(first, tail, rolled[:SUBLANES])
    return jnp.concatenate([head, rolled[SUBLANES:]], axis=0)


def _mix_out_kernel(x_ref, y_ref, g_ref, wc_ref, wgc_ref, wga_ref, bgc_ref, bga_ref, cw_ref, wco_ref, wao_ref,
                    wo_ref, o_ref, tail_ref, conv_y_ref, merged_ref, *, cw_width, chunk):
    @pl.when(pl.program_id(1) == 0)
    def _():
        tail_ref[...] = jnp.zeros_like(tail_ref)

    x = x_ref[...]
    tm, d = x.shape
    h = _rmsnorm(x, g_ref[...]).astype(BF16)
    for c in range(0, cw_width, chunk):
        cols = slice(c, c + chunk)
        cb = _dot(h, wc_ref[:, c:c + chunk])
        cc = _dot(h, wc_ref[:, cw_width + c:cw_width + c + chunk])
        cx = _dot(h, wc_ref[:, 2 * cw_width + c:2 * cw_width + c + chunk])
        u = cc * cx
        prev = tail_ref[:, cols]
        tail_ref[:, cols] = u[tm - SUBLANES:]
        cw = cw_ref[:, cols]
        conv = cw[0:1] * _shift_rows(u, prev, 2) + cw[1:2] * _shift_rows(u, prev, 1) + cw[2:3] * u
        conv_y_ref[:, cols] = (cb * conv).astype(BF16)
    for c in range(0, d, chunk):
        cols = slice(c, c + chunk)
        g_conv = jax.nn.sigmoid(_dot(h, wgc_ref[:, cols]) + bgc_ref[:, cols])
        g_attn = jax.nn.sigmoid(_dot(h, wga_ref[:, cols]) + bga_ref[:, cols])
        merged = g_conv * _dot(conv_y_ref[...], wco_ref[:, cols]) + g_attn * _dot(y_ref[...], wao_ref[:, cols])
        merged_ref[:, cols] = merged.astype(BF16)
    o_ref[...] = x + _dot(merged_ref[...], wo_ref[...])


def _mix_out(x, attn_y, g, w_in, b_gate, conv_w, w_conv_o, w_attn_o, w_out, *, seq, tm=1024, chunk=256):
    n, d = x.shape
    cw_width = w_conv_o.shape[0]
    aw = attn_y.shape[1]
    gate_col = (3 * cw_width + 3 * aw) // d
    assert seq % tm == 0 and n % seq == 0 and conv_w.shape == (CONV_K, cw_width)
    assert w_in.shape == (d, (gate_col + 2) * d) and cw_width % chunk == 0 and d % chunk == 0
    tiles = seq // tm
    row = lambda width: pl.BlockSpec((tm, width), lambda bi, j: (bi * tiles + j, 0))
    return pl.pallas_call(
        functools.partial(_mix_out_kernel, cw_width=cw_width, chunk=chunk),
        grid=(n // seq, tiles),
        in_specs=[row(d), row(aw), _resident((1, d)), _resident((d, 3 * cw_width)),
                  _resident((d, d), gate_col), _resident((d, d), gate_col + 1),
                  _resident((1, d)), _resident((1, d), 1), _resident((CONV_K, cw_width)),
                  _resident((cw_width, d)), _resident((aw, d)), _resident((d, d))],
        out_specs=row(d),
        out_shape=jax.ShapeDtypeStruct((n, d), F32),
        scratch_shapes=[pltpu.VMEM((SUBLANES, cw_width), F32), pltpu.VMEM((tm, cw_width), BF16),
                        pltpu.VMEM((tm, d), BF16)],
        compiler_params=pltpu.CompilerParams(
            dimension_semantics=("arbitrary", "arbitrary"), vmem_limit_bytes=48 * MIB),
        name="mix_out",
    )(x, attn_y, g, w_in, w_in, w_in, b_gate, b_gate, conv_w, w_conv_o, w_attn_o, w_out)


def kernel(x, ffn1_norm, ffn1_w13, ffn1_w2, mix_norm, w_in, b_gate, conv_w, w_conv_o, w_attn_o, w_out,
           ffn2_norm, ffn2_w13, ffn2_w2, final_norm):
    b, s, d = x.shape
    depth = w_in.shape[0]
    cw_width = w_conv_o.shape[1]
    aw = w_attn_o.shape[1]
    assert cw_width == aw and (3 * cw_width + 3 * aw) % d == 0
    gf = final_norm.reshape(1, d)
    xf = x.reshape(b * s, d)
    for l in range(depth):
        w_in_l = w_in[l].astype(BF16)
        xf = _ffn(xf, ffn1_norm[l].reshape(1, d), ffn1_w13, ffn1_w2, gf, layer=l, final_norm=False)
        q, k, v = _qkv(xf, mix_norm[l].reshape(1, d), w_in_l, width=aw, col_block=1)
        attn_y = _attn(q.reshape(b, s, aw), k.reshape(b, s, aw), v.reshape(b, s, aw)).reshape(b * s, aw)
        xf = _mix_out(xf, attn_y, mix_norm[l].reshape(1, d), w_in_l,
                      b_gate[l].reshape(1, 2 * d), conv_w[l], w_conv_o[l].astype(BF16),
                      w_attn_o[l].astype(BF16), w_out[l].astype(BF16), seq=s)
        xf = _ffn(xf, ffn2_norm[l].reshape(1, d), ffn2_w13, ffn2_w2, gf, layer=l, final_norm=(l == depth - 1))
    return xf.reshape(b, s, d)
```

```python
import functools
import math

import jax
import jax.numpy as jnp
from jax import lax
from jax.experimental import pallas as pl
from jax.experimental.pallas import tpu as pltpu

F32 = jnp.float32
BF16 = jnp.bfloat16

RMS_EPS = 1e-6
LOG2E = 1.4426950408889634
DEAD_LOG2 = -160.0
PARKED_LOG2 = -1e30
HEAD_DIM = 64
CONV_K = 3
LANES = 128
SUBLANES = 8
W13_ROWS = 64
W_SLOTS = 4
MIB = 1024 * 1024


def _resident(shape, col_block=0, layer=None):
    index = (0,) * (len(shape) - 1) + (col_block,)
    if layer is not None:
        shape, index = (None,) + tuple(shape), (layer,) + index
    return pl.BlockSpec(shape, lambda *_: index, pipeline_mode=pl.Buffered(1))


def _rmsnorm(x, g):
    ms = jnp.mean(x * x, axis=-1, keepdims=True)
    return x * lax.rsqrt(ms + RMS_EPS) * g


def _dot(a, b):
    return jnp.dot(a, b, preferred_element_type=F32)


def _load_as_bf16(chunks, stage_ref, sem):
    slots = stage_ref.shape[0]

    def copy(k):
        return pltpu.make_async_copy(chunks[k][0], stage_ref.at[k % slots], sem.at[k % slots])

    for k in range(min(slots - 1, len(chunks))):
        copy(k).start()
    for k in range(len(chunks)):
        if k + slots - 1 < len(chunks):
            copy(k + slots - 1).start()
        copy(k).wait()
        chunks[k][1][...] = stage_ref[k % slots].astype(BF16)


def _ffn_kernel(x_ref, g_ref, w13_hbm, w2_hbm, gf_ref, o_ref, w13_ref, w2_ref, act_ref, stage13_ref, stage2_ref,
                sem, *, layer, d_ff, ff_chunk, final_norm):
    @pl.when(pl.program_id(0) == 0)
    def _():
        rows13, rows2 = stage13_ref.shape[1], stage2_ref.shape[1]
        _load_as_bf16([(w13_hbm.at[layer, pl.ds(r, rows13), :], w13_ref.at[pl.ds(r, rows13), :])
                       for r in range(0, w13_ref.shape[0], rows13)], stage13_ref, sem)
        _load_as_bf16([(w2_hbm.at[layer, pl.ds(r, rows2), :], w2_ref.at[pl.ds(r, rows2), :])
                       for r in range(0, d_ff, rows2)], stage2_ref, sem)

    x = x_ref[...]
    h = _rmsnorm(x, g_ref[...]).astype(BF16)
    for c in range(0, d_ff, ff_chunk):
        a = _dot(h, w13_ref[:, c:c + ff_chunk])
        b = _dot(h, w13_ref[:, d_ff + c:d_ff + c + ff_chunk])
        act_ref[:, c:c + ff_chunk] = (a * jax.nn.sigmoid(a) * b).astype(BF16)
    y = x + 0.5 * _dot(act_ref[...], w2_ref[...])
    if final_norm:
        y = _rmsnorm(y, gf_ref[...])
    o_ref[...] = y


def _ffn(x, g, w13, w2, gf, *, layer, final_norm, tm=1024, ff_chunk=256):
    n, d = x.shape
    d_ff = w2.shape[1]
    rows13, rows2 = W13_ROWS, W13_ROWS * 2 * d_ff // d
    assert n % tm == 0 and d_ff % ff_chunk == 0 and w13.shape[1:] == (d, 2 * d_ff)
    assert d % rows13 == 0 and d_ff % rows2 == 0 and rows13 % 16 == 0 and rows2 % 16 == 0
    row = pl.BlockSpec((tm, d), lambda i: (i, 0))
    hbm = pl.BlockSpec(memory_space=pl.ANY)
    return pl.pallas_call(
        functools.partial(_ffn_kernel, layer=layer, d_ff=d_ff, ff_chunk=ff_chunk, final_norm=final_norm),
        grid=(n // tm,),
        in_specs=[row, _resident((1, d)), hbm, hbm, _resident((1, d))],
        out_specs=row,
        out_shape=jax.ShapeDtypeStruct((n, d), F32),
        scratch_shapes=[pltpu.VMEM((d, 2 * d_ff), BF16), pltpu.VMEM((d_ff, d), BF16), pltpu.VMEM((tm, d_ff), BF16),
                        pltpu.VMEM((W_SLOTS, rows13, 2 * d_ff), F32), pltpu.VMEM((W_SLOTS, rows2, d), F32),
                        pltpu.SemaphoreType.DMA((W_SLOTS,))],
        compiler_params=pltpu.CompilerParams(
            dimension_semantics=("arbitrary",), vmem_limit_bytes=56 * MIB),
        name="ffn",
    )(x, g, w13, w2, gf)


def _qkv_kernel(x_ref, g_ref, w_ref, q_ref, k_ref, v_ref, *, width, scale):
    h = _rmsnorm(x_ref[...], g_ref[...]).astype(BF16)
    q_ref[...] = (_dot(h, w_ref[:, 0:width]) * scale).astype(BF16)
    k_ref[...] = _dot(h, w_ref[:, width:2 * width]).astype(BF16)
    v_ref[...] = _dot(h, w_ref[:, 2 * width:3 * width]).astype(BF16)


def _qkv(x, g, w_in, *, layer, width, col_block, tm=1024):
    n, d = x.shape
    assert n % tm == 0
    out = jax.ShapeDtypeStruct((n, width), BF16)
    out_spec = pl.BlockSpec((tm, width), lambda i: (i, 0))
    return pl.pallas_call(
        functools.partial(_qkv_kernel, width=width, scale=LOG2E / math.sqrt(HEAD_DIM)),
        grid=(n // tm,),
        in_specs=[pl.BlockSpec((tm, d), lambda i: (i, 0)), _resident((1, d)),
                  _resident((d, 3 * width), col_block, layer)],
        out_specs=[out_spec, out_spec, out_spec],
        out_shape=[out, out, out],
        compiler_params=pltpu.CompilerParams(
            dimension_semantics=("parallel",), vmem_limit_bytes=32 * MIB),
        name="qkv",
    )(x, g, w_in)


def _attn_kernel(q_ref, k_ref, v_ref, w_ref, o_ref, qm_ref, carry_ref, acc_ref, *, tq, pairs, group):
    i0 = pl.program_id(1) * group
    first_head = lax.broadcasted_iota(jnp.int32, (tq, LANES), 1) < HEAD_DIM
    zeros = jnp.zeros((tq, LANES), BF16)
    lanes = [slice(p * LANES, (p + 1) * LANES) for p in range(pairs)]
    chains = [(g, p) for g in range(group) for p in range(pairs)]

    def split_heads(t):
        return jnp.concatenate([jnp.where(first_head, t, zeros), jnp.where(first_head, zeros, t)], axis=0)

    row = lax.broadcasted_iota(jnp.int32, (2 * tq, tq), 0) & (tq - 1)
    col = lax.broadcasted_iota(jnp.int32, (2 * tq, tq), 1)
    causal = col < row

    def block(j, diagonal):
        starts = [pl.multiple_of(jnp.maximum(i0 + g - j, 0) * tq, tq) for g in range(group)]
        zs = [lax.dot_general(qm_ref[g, p], k_ref[0, pl.ds(starts[g], tq), lanes[p]],
                              (((1,), (1,)), ((), ())), preferred_element_type=F32)
              for g, p in chains]
        log_betas, sums = [], []
        for z in zs:
            neg_part = jnp.minimum(z, 0.0)
            neg_relu = neg_part - z
            sp = jnp.log2(1.0 + jnp.exp2(neg_part + neg_relu))
            log_beta = neg_part - sp
            log_keep = neg_relu - sp
            if diagonal:
                log_keep = jnp.where(causal, log_keep, 0.0)
            hi = log_keep.astype(BF16)
            lo = (log_keep - hi.astype(F32)).astype(BF16)
            log_betas.append(log_beta)
            sums.append(_dot(jnp.concatenate([hi, lo], axis=1), w_ref[...]))
        for n, (g, p) in enumerate(chains):
            carry = carry_ref[g, p]
            a = jnp.exp2(log_betas[n] + (carry + sums[n][:, :tq]))
            if diagonal:
                a = jnp.where(causal, a, 0.0)
            carry_ref[g, p] = carry + sums[n][:, tq:]
            ab = a.astype(BF16)
            a_pair = jnp.concatenate([ab[:tq], ab[tq:]], axis=1)
            acc_ref[g, :, lanes[p]] += _dot(a_pair, split_heads(v_ref[0, pl.ds(starts[g], tq), lanes[p]]))

    for g, p in chains:
        qm_ref[g, p] = split_heads(q_ref[0, g * tq:(g + 1) * tq, lanes[p]])
    carry_ref[...] = jnp.zeros_like(carry_ref)
    acc_ref[...] = jnp.zeros_like(acc_ref)
    block(0, True)

    def any_row_alive():
        live = carry_ref[0, 0]
        for g, p in chains[1:]:
            live = jnp.maximum(live, carry_ref[g, p])
        return (jnp.max(live) > DEAD_LOG2).astype(jnp.int32)

    def body(state):
        j, _ = state
        for g in range(group - 1):
            @pl.when(i0 + g - j < 0)
            def _():
                carry_ref[g] = jnp.full(carry_ref.shape[1:], PARKED_LOG2, F32)
        block(j, False)
        return j + 1, any_row_alive()

    lax.while_loop(lambda state: (i0 + group - 1 - state[0] >= 0) & (state[1] > 0), body, (1, jnp.int32(1)))
    for g in range(group):
        o_ref[0, g * tq:(g + 1) * tq, :] = acc_ref[g].astype(BF16)


def _suffix_sum_matrix(tq):
    j = jnp.arange(2 * tq)[:, None] % tq
    s = jnp.arange(2 * tq)[None, :]
    return jnp.where((s >= tq) | (j > s), 1.0, 0.0).astype(BF16)


def _attn(q, k, v, *, tq=LANES, group=2):
    b, s, width = q.shape
    pairs = width // LANES
    assert tq == LANES and s % (group * tq) == 0 and width % LANES == 0
    q_spec = pl.BlockSpec((1, group * tq, width), lambda bi, i: (bi, i, 0))
    kv_spec = pl.BlockSpec((1, s, width), lambda bi, i: (bi, 0, 0))
    return pl.pallas_call(
        functools.partial(_attn_kernel, tq=tq, pairs=pairs, group=group),
        grid=(b, s // (group * tq)),
        in_specs=[q_spec, kv_spec, kv_spec, _resident((2 * tq, 2 * tq))],
        out_specs=q_spec,
        out_shape=jax.ShapeDtypeStruct((b, s, width), BF16),
        scratch_shapes=[pltpu.VMEM((group, pairs, 2 * tq, LANES), BF16),
                        pltpu.VMEM((group, pairs, 2 * tq, tq), F32),
                        pltpu.VMEM((group, tq, width), F32)],
        compiler_params=pltpu.CompilerParams(
            dimension_semantics=("parallel", "parallel"), vmem_limit_bytes=32 * MIB),
        name="attn",
    )(q, k, v, _suffix_sum_matrix(tq))


def _shift_rows(u, prev, shift):
    rolled = pltpu.roll(u, shift, axis=0)
    tail = pltpu.roll(prev, shift, axis=0)
    first = lax.broadcasted_iota(jnp.int32, tail.shape, 0) < shift
    head = jnp.where(first, tail, rolled[:SUBLANES])
    return jnp.concatenate([head, rolled[SUBLANES:]], axis=0)


def _mix_out_kernel(x_ref, y_ref, g_ref, wc_ref, wgc_ref, wga_ref, bgc_ref, bga_ref, cw_ref, wco_ref, wao_ref,
                    wo_ref, o_ref, tail_ref, conv_y_ref, merged_ref, *, cw_width, chunk):
    @pl.when(pl.program_id(1) == 0)
    def _():
        tail_ref[...] = jnp.zeros_like(tail_ref)

    x = x_ref[...]
    tm, d = x.shape
    h = _rmsnorm(x, g_ref[...]).astype(BF16)
    for c in range(0, cw_width, chunk):
        cols = slice(c, c + chunk)
        cb = _dot(h, wc_ref[:, c:c + chunk])
        cc = _dot(h, wc_ref[:, cw_width + c:cw_width + c + chunk])
        cx = _dot(h, wc_ref[:, 2 * cw_width + c:2 * cw_width + c + chunk])
        u = cc * cx
        prev = tail_ref[:, cols]
        tail_ref[:, cols] = u[tm - SUBLANES:]
        cw = cw_ref[:, cols]
        conv = cw[0:1] * _shift_rows(u, prev, 2) + cw[1:2] * _shift_rows(u, prev, 1) + cw[2:3] * u
        conv_y_ref[:, cols] = (cb * conv).astype(BF16)
    for c in range(0, d, chunk):
        cols = slice(c, c + chunk)
        g_conv = jax.nn.sigmoid(_dot(h, wgc_ref[:, cols]) + bgc_ref[:, cols])
        g_attn = jax.nn.sigmoid(_dot(h, wga_ref[:, cols]) + bga_ref[:, cols])
        merged = g_conv * _dot(conv_y_ref[...], wco_ref[:, cols]) + g_attn * _dot(y_ref[...], wao_ref[:, cols])
        merged_ref[:, cols] = merged.astype(BF16)
    o_ref[...] = x + _dot(merged_ref[...], wo_ref[...])


def _mix_out(x, attn_y, g, w_in, b_gate, conv_w, w_conv_o, w_attn_o, w_out, *, layer, seq, tm=1024, chunk=256):
    n, d = x.shape
    cw_width = w_conv_o.shape[0]
    aw = attn_y.shape[1]
    gate_col = (3 * cw_width + 3 * aw) // d
    assert seq % tm == 0 and n % seq == 0 and conv_w.shape == (CONV_K, cw_width)
    assert w_in.shape[1:] == (d, (gate_col + 2) * d) and cw_width % chunk == 0 and d % chunk == 0
    tiles = seq // tm
    row = lambda width: pl.BlockSpec((tm, width), lambda bi, j: (bi * tiles + j, 0))
    return pl.pallas_call(
        functools.partial(_mix_out_kernel, cw_width=cw_width, chunk=chunk),
        grid=(n // seq, tiles),
        in_specs=[row(d), row(aw), _resident((1, d)), _resident((d, 3 * cw_width), 0, layer),
                  _resident((d, d), gate_col, layer), _resident((d, d), gate_col + 1, layer),
                  _resident((1, d)), _resident((1, d), 1), _resident((CONV_K, cw_width)),
                  _resident((cw_width, d)), _resident((aw, d)), _resident((d, d))],
        out_specs=row(d),
        out_shape=jax.ShapeDtypeStruct((n, d), F32),
        scratch_shapes=[pltpu.VMEM((SUBLANES, cw_width), F32), pltpu.VMEM((tm, cw_width), BF16),
                        pltpu.VMEM((tm, d), BF16)],
        compiler_params=pltpu.CompilerParams(
            dimension_semantics=("arbitrary", "arbitrary"), vmem_limit_bytes=48 * MIB),
        name="mix_out",
    )(x, attn_y, g, w_in, w_in, w_in, b_gate, b_gate, conv_w, w_conv_o, w_attn_o, w_out)


def kernel(x, ffn1_norm, ffn1_w13, ffn1_w2, mix_norm, w_in, b_gate, conv_w, w_conv_o, w_attn_o, w_out,
           ffn2_norm, ffn2_w13, ffn2_w2, final_norm):
    b, s, d = x.shape
    depth = w_in.shape[0]
    cw_width = w_conv_o.shape[1]
    aw = w_attn_o.shape[1]
    assert cw_width == aw and (3 * cw_width + 3 * aw) % d == 0
    gf = final_norm.reshape(1, d)
    xf = x.reshape(b * s, d)
    w_in_bf = w_in.astype(BF16)
    for l in range(depth):
        xf = _ffn(xf, ffn1_norm[l].reshape(1, d), ffn1_w13, ffn1_w2, gf, layer=l, final_norm=False)
        q, k, v = _qkv(xf, mix_norm[l].reshape(1, d), w_in_bf, layer=l, width=aw, col_block=1)
        attn_y = _attn(q.reshape(b, s, aw), k.reshape(b, s, aw), v.reshape(b, s, aw)).reshape(b * s, aw)
        xf = _mix_out(xf, attn_y, mix_norm[l].reshape(1, d), w_in_bf,
                      b_gate[l].reshape(1, 2 * d), conv_w[l], w_conv_o[l].astype(BF16),
                      w_attn_o[l].astype(BF16), w_out[l].astype(BF16), layer=l, seq=s)
        xf = _ffn(xf, ffn2_norm[l].reshape(1, d), ffn2_w13, ffn2_w2, gf, layer=l, final_norm=(l == depth - 1))
    return xf.reshape(b, s, d)
```

```python
import functools
import math

import jax
import jax.numpy as jnp
from jax import lax
from jax.experimental import pallas as pl
from jax.experimental.pallas import tpu as pltpu

F32 = jnp.float32
BF16 = jnp.bfloat16

RMS_EPS = 1e-6
LOG2E = 1.4426950408889634
DEAD_LOG2 = -160.0
PARKED_LOG2 = -1e30
HEAD_DIM = 64
CONV_K = 3
LANES = 128
SUBLANES = 8
W13_ROWS = 64
W_SLOTS = 4
MIB = 1024 * 1024


def _resident(shape, col_block=0, layer=None):
    index = (0,) * (len(shape) - 1) + (col_block,)
    if layer is not None:
        shape, index = (None,) + tuple(shape), (layer,) + index
    return pl.BlockSpec(shape, lambda *_: index, pipeline_mode=pl.Buffered(1))


def _rmsnorm(x, g):
    ms = jnp.mean(x * x, axis=-1, keepdims=True)
    return x * lax.rsqrt(ms + RMS_EPS) * g


def _dot(a, b):
    return jnp.dot(a, b, preferred_element_type=F32)


def _load_as_bf16(chunks, stage_ref, sem):
    slots = stage_ref.shape[0]

    def copy(k):
        return pltpu.make_async_copy(chunks[k][0], stage_ref.at[k % slots], sem.at[k % slots])

    for k in range(min(slots - 1, len(chunks))):
        copy(k).start()
    for k in range(len(chunks)):
        if k + slots - 1 < len(chunks):
            copy(k + slots - 1).start()
        copy(k).wait()
        chunks[k][1][...] = stage_ref[k % slots].astype(BF16)


def _ffn_kernel(x_ref, g_ref, w13_hbm, w2_hbm, gf_ref, o_ref, w13_ref, w2_ref, act_ref, stage13_ref, stage2_ref,
                sem, *, layer, d_ff, ff_chunk, final_norm):
    @pl.when(pl.program_id(0) == 0)
    def _():
        rows13, rows2 = stage13_ref.shape[1], stage2_ref.shape[1]
        _load_as_bf16([(w13_hbm.at[layer, pl.ds(r, rows13), :], w13_ref.at[pl.ds(r, rows13), :])
                       for r in range(0, w13_ref.shape[0], rows13)], stage13_ref, sem)
        _load_as_bf16([(w2_hbm.at[layer, pl.ds(r, rows2), :], w2_ref.at[pl.ds(r, rows2), :])
                       for r in range(0, d_ff, rows2)], stage2_ref, sem)

    x = x_ref[...]
    h = _rmsnorm(x, g_ref[...]).astype(BF16)
    for c in range(0, d_ff, ff_chunk):
        a = _dot(h, w13_ref[:, c:c + ff_chunk])
        b = _dot(h, w13_ref[:, d_ff + c:d_ff + c + ff_chunk])
        act_ref[:, c:c + ff_chunk] = (a * jax.nn.sigmoid(a) * b).astype(BF16)
    y = x + 0.5 * _dot(act_ref[...], w2_ref[...])
    if final_norm:
        y = _rmsnorm(y, gf_ref[...])
    o_ref[...] = y


def _ffn(x, g, w13, w2, gf, *, layer, final_norm, tm=1024, ff_chunk=256):
    n, d = x.shape
    d_ff = w2.shape[1]
    rows13, rows2 = W13_ROWS, W13_ROWS * 2 * d_ff // d
    assert n % tm == 0 and d_ff % ff_chunk == 0 and w13.shape[1:] == (d, 2 * d_ff)
    assert d % rows13 == 0 and d_ff % rows2 == 0 and rows13 % 16 == 0 and rows2 % 16 == 0
    row = pl.BlockSpec((tm, d), lambda i: (i, 0))
    hbm = pl.BlockSpec(memory_space=pl.ANY)
    return pl.pallas_call(
        functools.partial(_ffn_kernel, layer=layer, d_ff=d_ff, ff_chunk=ff_chunk, final_norm=final_norm),
        grid=(n // tm,),
        in_specs=[row, _resident((1, d)), hbm, hbm, _resident((1, d))],
        out_specs=row,
        out_shape=jax.ShapeDtypeStruct((n, d), F32),
        scratch_shapes=[pltpu.VMEM((d, 2 * d_ff), BF16), pltpu.VMEM((d_ff, d), BF16), pltpu.VMEM((tm, d_ff), BF16),
                        pltpu.VMEM((W_SLOTS, rows13, 2 * d_ff), F32), pltpu.VMEM((W_SLOTS, rows2, d), F32),
                        pltpu.SemaphoreType.DMA((W_SLOTS,))],
        compiler_params=pltpu.CompilerParams(
            dimension_semantics=("arbitrary",), vmem_limit_bytes=56 * MIB),
        name="ffn",
    )(x, g, w13, w2, gf)


def _qkv_kernel(x_ref, g_ref, w_ref, q_ref, k_ref, v_ref, *, width, scale):
    h = _rmsnorm(x_ref[...], g_ref[...]).astype(BF16)
    q_ref[...] = (_dot(h, w_ref[:, 0:width]) * scale).astype(BF16)
    k_ref[...] = _dot(h, w_ref[:, width:2 * width]).astype(BF16)
    v_ref[...] = _dot(h, w_ref[:, 2 * width:3 * width]).astype(BF16)


def _qkv(x, g, w_in, *, layer, width, col_block, tm=1024):
    n, d = x.shape
    assert n % tm == 0
    out = jax.ShapeDtypeStruct((n, width), BF16)
    out_spec = pl.BlockSpec((tm, width), lambda i: (i, 0))
    return pl.pallas_call(
        functools.partial(_qkv_kernel, width=width, scale=LOG2E / math.sqrt(HEAD_DIM)),
        grid=(n // tm,),
        in_specs=[pl.BlockSpec((tm, d), lambda i: (i, 0)), _resident((1, d)),
                  _resident((d, 3 * width), col_block, layer)],
        out_specs=[out_spec, out_spec, out_spec],
        out_shape=[out, out, out],
        compiler_params=pltpu.CompilerParams(
            dimension_semantics=("parallel",), vmem_limit_bytes=32 * MIB),
        name="qkv",
    )(x, g, w_in)


def _attn_kernel(q_ref, k_ref, v_ref, w_ref, o_ref, qm_ref, carry_ref, acc_ref, *, tq, pairs, group):
    i0 = pl.program_id(1) * group
    first_head = lax.broadcasted_iota(jnp.int32, (tq, LANES), 1) < HEAD_DIM
    zeros = jnp.zeros((tq, LANES), BF16)
    lanes = [slice(p * LANES, (p + 1) * LANES) for p in range(pairs)]
    chains = [(g, p) for g in range(group) for p in range(pairs)]

    def split_heads(t):
        return jnp.concatenate([jnp.where(first_head, t, zeros), jnp.where(first_head, zeros, t)], axis=0)

    row = lax.broadcasted_iota(jnp.int32, (2 * tq, tq), 0) & (tq - 1)
    col = lax.broadcasted_iota(jnp.int32, (2 * tq, tq), 1)
    causal = col < row

    def block(j, diagonal):
        starts = [pl.multiple_of(jnp.maximum(i0 + g - j, 0) * tq, tq) for g in range(group)]
        zs = [lax.dot_general(qm_ref[g, p], k_ref[0, pl.ds(starts[g], tq), lanes[p]],
                              (((1,), (1,)), ((), ())), preferred_element_type=F32)
              for g, p in chains]
        log_betas, sums = [], []
        for z in zs:
            neg_part = jnp.minimum(z, 0.0)
            neg_relu = neg_part - z
            sp = jnp.log2(1.0 + jnp.exp2(neg_part + neg_relu))
            log_beta = neg_part - sp
            log_keep = neg_relu - sp
            if diagonal:
                log_keep = jnp.where(causal, log_keep, 0.0)
            hi = log_keep.astype(BF16)
            lo = (log_keep - hi.astype(F32)).astype(BF16)
            log_betas.append(log_beta)
            sums.append(_dot(jnp.concatenate([hi, lo], axis=1), w_ref[...]))
        for n, (g, p) in enumerate(chains):
            carry = carry_ref[g, p]
            a = jnp.exp2(log_betas[n] + (carry + sums[n][:, :tq]))
            if diagonal:
                a = jnp.where(causal, a, 0.0)
            carry_ref[g, p] = carry + sums[n][:, tq:]
            ab = a.astype(BF16)
            a_pair = jnp.concatenate([ab[:tq], ab[tq:]], axis=1)
            acc_ref[g, :, lanes[p]] += _dot(a_pair, split_heads(v_ref[0, pl.ds(starts[g], tq), lanes[p]]))

    for g, p in chains:
        qm_ref[g, p] = split_heads(q_ref[0, g * tq:(g + 1) * tq, lanes[p]])
    carry_ref[...] = jnp.zeros_like(carry_ref)
    acc_ref[...] = jnp.zeros_like(acc_ref)
    block(0, True)

    def any_row_alive():
        live = carry_ref[0, 0]
        for g, p in chains[1:]:
            live = jnp.maximum(live, carry_ref[g, p])
        return (jnp.max(live) > DEAD_LOG2).astype(jnp.int32)

    def body(state):
        j, _ = state
        for g in range(group - 1):
            @pl.when(i0 + g - j < 0)
            def _():
                carry_ref[g] = jnp.full(carry_ref.shape[1:], PARKED_LOG2, F32)
        block(j, False)
        return j + 1, any_row_alive()

    lax.while_loop(lambda state: (i0 + group - 1 - state[0] >= 0) & (state[1] > 0), body, (1, jnp.int32(1)))
    for g in range(group):
        o_ref[0, g * tq:(g + 1) * tq, :] = acc_ref[g].astype(BF16)


def _suffix_sum_matrix(tq):
    j = jnp.arange(2 * tq)[:, None] % tq
    s = jnp.arange(2 * tq)[None, :]
    return jnp.where((s >= tq) | (j > s), 1.0, 0.0).astype(BF16)


def _attn(q, k, v, *, tq=LANES, group=2):
    b, s, width = q.shape
    pairs = width // LANES
    assert tq == LANES and s % (group * tq) == 0 and width % LANES == 0
    q_spec = pl.BlockSpec((1, group * tq, width), lambda bi, i: (bi, i, 0))
    kv_spec = pl.BlockSpec((1, s, width), lambda bi, i: (bi, 0, 0))
    return pl.pallas_call(
        functools.partial(_attn_kernel, tq=tq, pairs=pairs, group=group),
        grid=(b, s // (group * tq)),
        in_specs=[q_spec, kv_spec, kv_spec, _resident((2 * tq, 2 * tq))],
        out_specs=q_spec,
        out_shape=jax.ShapeDtypeStruct((b, s, width), BF16),
        scratch_shapes=[pltpu.VMEM((group, pairs, 2 * tq, LANES), BF16),
                        pltpu.VMEM((group, pairs, 2 * tq, tq), F32),
                        pltpu.VMEM((group, tq, width), F32)],
        compiler_params=pltpu.CompilerParams(
            dimension_semantics=("parallel", "parallel"), vmem_limit_bytes=32 * MIB),
        name="attn",
    )(q, k, v, _suffix_sum_matrix(tq))


def _shift_rows(u, prev, shift):
    rolled = pltpu.roll(u, shift, axis=0)
    tail = pltpu.roll(prev, shift, axis=0)
    first = lax.broadcasted_iota(jnp.int32, tail.shape, 0) < shift
    head = jnp.where(first, tail, rolled[:SUBLANES])
    return jnp.concatenate([head, rolled[SUBLANES:]], axis=0)


def _mix_out_kernel(x_ref, y_ref, g_ref, wc_ref, wgc_ref, wga_ref, bgc_ref, bga_ref, cw_ref, wco_ref, wao_ref,
                    wo_ref, o_ref, tail_ref, conv_y_ref, merged_ref, *, cw_width, chunk):
    @pl.when(pl.program_id(1) == 0)
    def _():
        tail_ref[...] = jnp.zeros_like(tail_ref)

    x = x_ref[...]
    tm, d = x.shape
    h = _rmsnorm(x, g_ref[...]).astype(BF16)
    for c in range(0, cw_width, chunk):
        cols = slice(c, c + chunk)
        cb = _dot(h, wc_ref[:, c:c + chunk])
        cc = _dot(h, wc_ref[:, cw_width + c:cw_width + c + chunk])
        cx = _dot(h, wc_ref[:, 2 * cw_width + c:2 * cw_width + c + chunk])
        u = cc * cx
        prev = tail_ref[:, cols]
        tail_ref[:, cols] = u[tm - SUBLANES:]
        cw = cw_ref[:, cols]
        conv = cw[0:1] * _shift_rows(u, prev, 2) + cw[1:2] * _shift_rows(u, prev, 1) + cw[2:3] * u
        conv_y_ref[:, cols] = (cb * conv).astype(BF16)
    for c in range(0, d, chunk):
        cols = slice(c, c + chunk)
        g_conv = jax.nn.sigmoid(_dot(h, wgc_ref[:, cols]) + bgc_ref[:, cols])
        g_attn = jax.nn.sigmoid(_dot(h, wga_ref[:, cols]) + bga_ref[:, cols])
        merged = g_conv * _dot(conv_y_ref[...], wco_ref[:, cols]) + g_attn * _dot(y_ref[...], wao_ref[:, cols])
        merged_ref[:, cols] = merged.astype(BF16)
    o_ref[...] = x + _dot(merged_ref[...], wo_ref[...])


def _mix_out(x, attn_y, g, w_in, b_gate, conv_w, w_conv_o, w_attn_o, w_out, *, layer, seq, tm=1024, chunk=256):
    n, d = x.shape
    cw_width = w_conv_o.shape[1]
    aw = attn_y.shape[1]
    gate_col = (3 * cw_width + 3 * aw) // d
    assert seq % tm == 0 and n % seq == 0 and conv_w.shape[1:] == (CONV_K, cw_width)
    assert w_in.shape[1:] == (d, (gate_col + 2) * d) and cw_width % chunk == 0 and d % chunk == 0
    tiles = seq // tm
    row = lambda width: pl.BlockSpec((tm, width), lambda bi, j: (bi * tiles + j, 0))
    return pl.pallas_call(
        functools.partial(_mix_out_kernel, cw_width=cw_width, chunk=chunk),
        grid=(n // seq, tiles),
        in_specs=[row(d), row(aw), _resident((1, d)), _resident((d, 3 * cw_width), 0, layer),
                  _resident((d, d), gate_col, layer), _resident((d, d), gate_col + 1, layer),
                  _resident((1, d), 0, layer), _resident((1, d), 1, layer), _resident((CONV_K, cw_width), 0, layer),
                  _resident((cw_width, d), 0, layer), _resident((aw, d), 0, layer), _resident((d, d), 0, layer)],
        out_specs=row(d),
        out_shape=jax.ShapeDtypeStruct((n, d), F32),
        scratch_shapes=[pltpu.VMEM((SUBLANES, cw_width), F32), pltpu.VMEM((tm, cw_width), BF16),
                        pltpu.VMEM((tm, d), BF16)],
        compiler_params=pltpu.CompilerParams(
            dimension_semantics=("arbitrary", "arbitrary"), vmem_limit_bytes=48 * MIB),
        name="mix_out",
    )(x, attn_y, g, w_in, w_in, w_in, b_gate, b_gate, conv_w, w_conv_o, w_attn_o, w_out)


def kernel(x, ffn1_norm, ffn1_w13, ffn1_w2, mix_norm, w_in, b_gate, conv_w, w_conv_o, w_attn_o, w_out,
           ffn2_norm, ffn2_w13, ffn2_w2, final_norm):
    b, s, d = x.shape
    depth = w_in.shape[0]
    cw_width = w_conv_o.shape[1]
    aw = w_attn_o.shape[1]
    assert cw_width == aw and (3 * cw_width + 3 * aw) % d == 0
    gf = final_norm.reshape(1, d)
    xf = x.reshape(b * s, d)
    w_in_bf = w_in.astype(BF16)
    w_conv_o_bf, w_attn_o_bf, w_out_bf = w_conv_o.astype(BF16), w_attn_o.astype(BF16), w_out.astype(BF16)
    b_gate3 = b_gate.reshape(depth, 1, 2 * d)
    for l in range(depth):
        xf = _ffn(xf, ffn1_norm[l].reshape(1, d), ffn1_w13, ffn1_w2, gf, layer=l, final_norm=False)
        q, k, v = _qkv(xf, mix_norm[l].reshape(1, d), w_in_bf, layer=l, width=aw, col_block=1)
        attn_y = _attn(q.reshape(b, s, aw), k.reshape(b, s, aw), v.reshape(b, s, aw)).reshape(b * s, aw)
        xf = _mix_out(xf, attn_y, mix_norm[l].reshape(1, d), w_in_bf, b_gate3, conv_w, w_conv_o_bf,
                      w_attn_o_bf, w_out_bf, layer=l, seq=s)
        xf = _ffn(xf, ffn2_norm[l].reshape(1, d), ffn2_w13, ffn2_w2, gf, layer=l, final_norm=(l == depth - 1))
    return xf.reshape(b, s, d)
```
